```python
import math
import jax, jax.numpy as jnp
from jax import lax
import numpy as np

D_MODEL = 1024
BATCH = 8
SEQ = 8192
DEPTH = 1
DEC_BATCH = 4
DEC_SEQ = 8192
PAST_LEN = 128

HEAD_DIM = 64
N_HEADS_A = D_MODEL // (2 * HEAD_DIM)
N_HEADS_B = D_MODEL // (2 * HEAD_DIM)
N_KV_B = N_HEADS_B // 4
WIDTH_A = N_HEADS_A * HEAD_DIM
WIDTH_B = N_HEADS_B * HEAD_DIM
MIX_WIDTH = WIDTH_A + WIDTH_B
KV_WIDTH_B = N_KV_B * HEAD_DIM
IN_COLS = 3 * WIDTH_A + WIDTH_B + 2 * KV_WIDTH_B
IN_SPLITS = (WIDTH_A, 2 * WIDTH_A, 3 * WIDTH_A, 3 * WIDTH_A + WIDTH_B, 3 * WIDTH_A + WIDTH_B + KV_WIDTH_B)
DILATED_PATTERNS = ((128, 1), (512, 4), (2048, 16))
N_MEM = 256
N_HEADS_MEM = 4
HEAD_DIM_MEM = D_MODEL // N_HEADS_MEM
D_FF = 2816
CONV_W = 3
GRID_W = 64
ROPE_THETA = 10000.0
Q_BLOCK = 128
LN_EPS = 1e-5
RMS_EPS = 1e-6

kernel_name = 'hybrid_dilated_axial_gqa_encoder'


def layer_norm(x, g, b):
    xf = x.astype(jnp.float32)
    mu = jnp.mean(xf, axis=-1, keepdims=True)
    var = jnp.mean(jnp.square(xf - mu), axis=-1, keepdims=True)
    return ((xf - mu) * lax.rsqrt(var + LN_EPS) * g + b).astype(x.dtype)


def rms_norm(x, g):
    xf = x.astype(jnp.float32)
    ms = jnp.mean(jnp.square(xf), axis=-1, keepdims=True)
    return (xf * lax.rsqrt(ms + RMS_EPS) * g).astype(x.dtype)


def rope_angles(pos, dim):
    inv = ROPE_THETA ** (-jnp.arange(0, dim, 2, dtype=jnp.float32) / dim)
    ang = pos.astype(jnp.float32)[:, None] * inv[None, :]
    return jnp.cos(ang), jnp.sin(ang)


def apply_rope(x, cos, sin):
    x1, x2 = jnp.split(x, 2, axis=-1)
    c = cos[:, None, :].astype(x.dtype)
    s = sin[:, None, :].astype(x.dtype)
    return jnp.concatenate([x1 * c - x2 * s, x2 * c + x1 * s], axis=-1)


def apply_axial_rope(x):
    n = x.shape[1]
    rows = n // GRID_W
    row = jnp.repeat(jnp.arange(rows), GRID_W)
    col = jnp.tile(jnp.arange(GRID_W), rows)
    half = x.shape[-1] // 2
    cr, sr = rope_angles(row, half)
    cc, sc = rope_angles(col, half)
    xr, xc = jnp.split(x, 2, axis=-1)
    return jnp.concatenate([apply_rope(xr, cr, sr), apply_rope(xc, cc, sc)], axis=-1)


def dilated_window_attention(q, k, v, window, dilation):
    b, n, h, e = q.shape
    half = window // (2 * dilation)
    blk = half
    L = n // dilation
    nb = -(-L // blk)
    lp = nb * blk

    def phases(t):
        return t.reshape(b, L, dilation, h, e).transpose(0, 2, 1, 3, 4)

    qp = jnp.pad(phases(q), ((0, 0), (0, 0), (0, lp - L), (0, 0), (0, 0))).reshape(b, dilation, nb, blk, h, e)

    def key_windows(t):
        tp = jnp.pad(phases(t), ((0, 0), (0, 0), (blk, lp - L + blk), (0, 0), (0, 0)))
        tp = tp.reshape(b, dilation, nb + 2, blk, h, e)
        return jnp.concatenate([tp[:, :, :nb], tp[:, :, 1:nb + 1], tp[:, :, 2:]], axis=3)

    kw = key_windows(k)
    vw = key_windows(v)
    qi = jnp.arange(nb)[:, None, None] * blk + jnp.arange(blk)[None, :, None]
    kj = (jnp.arange(nb)[:, None, None] - 1) * blk + jnp.arange(3 * blk)[None, None, :]
    valid = ((jnp.abs(kj - qi) <= half) & (kj >= 0) & (kj < L)) | (kj == qi)

    s = jnp.einsum('bdnqhe,bdnkhe->bdnhqk', qp, kw).astype(jnp.float32) * (e ** -0.5)
    s = jnp.where(valid[None, None, :, None], s, -jnp.inf)
    m = jnp.max(s, axis=-1, keepdims=True)
    p = jnp.exp(s - m)
    den = jnp.sum(p, axis=-1, keepdims=True)
    o = jnp.einsum('bdnhqk,bdnkhe->bdnqhe', (p / den).astype(v.dtype), vw)
    lse = (m + jnp.log(den))[..., 0]
    o = o.reshape(b, dilation, lp, h, e)[:, :, :L].transpose(0, 2, 1, 3, 4).reshape(b, n, h, e)
    lse = lse.transpose(0, 1, 2, 4, 3).reshape(b, dilation, lp, h)[:, :, :L]
    lse = lse.transpose(0, 2, 1, 3).reshape(b, n, h)
    return o, lse


def gqa_blocked(q, k, v):
    b, n, hq, e = q.shape
    hk = k.shape[2]
    g = hq // hk
    nblk = n // Q_BLOCK
    qb = q.reshape(b, nblk, Q_BLOCK, hk, g, e).transpose(1, 0, 2, 3, 4, 5)

    def one_block(qblk):
        s = jnp.einsum('bqkge,bske->bkgqs', qblk, k).astype(jnp.float32) * (e ** -0.5)
        p = jax.nn.softmax(s, axis=-1)
        return jnp.einsum('bkgqs,bske->bqkge', p.astype(v.dtype), v)

    o = lax.map(one_block, qb)
    return o.transpose(1, 0, 2, 3, 4, 5).reshape(b, n, hq, e)


def memory_cross_attention(x, mem, wq, wk, wv, wo):
    b, n, _ = x.shape
    m = mem.shape[1]
    q = (x @ wq).reshape(b, n, N_HEADS_MEM, HEAD_DIM_MEM)
    k = (mem @ wk).reshape(b, m, N_HEADS_MEM, HEAD_DIM_MEM)
    v = (mem @ wv).reshape(b, m, N_HEADS_MEM, HEAD_DIM_MEM)
    s = jnp.einsum('bnhe,bmhe->bhnm', q, k).astype(jnp.float32) * (HEAD_DIM_MEM ** -0.5)
    p = jax.nn.softmax(s, axis=-1)
    o = jnp.einsum('bhnm,bmhe->bnhe', p.astype(v.dtype), v).reshape(b, n, D_MODEL)
    return o @ wo


def conv_glu_ffn(x, w_up, conv_w, conv_b, w_down):
    u = x @ w_up
    gate, val = jnp.split(u, 2, axis=-1)
    gate = lax.conv_general_dilated(
        gate, conv_w[:, None, :], window_strides=(1,),
        padding=((CONV_W // 2, CONV_W // 2),),
        dimension_numbers=('NWC', 'WIO', 'NWC'),
        feature_group_count=D_FF) + conv_b
    h = jax.nn.gelu(gate, approximate=False) * val
    return h @ w_down


def encoder(x, mem, w_in, q_norm_g, k_norm_g, out_norm_a, out_norm_b, w_o, ln1_g, ln1_b,
            wc_q, wc_k, wc_v, wc_o, ln2_g, ln2_b, w_up, conv_w, conv_b, w_down, ln3_g, ln3_b):
    b, n, _ = x.shape
    alpha = (2 * DEPTH) ** 0.25
    cos, sin = rope_angles(jnp.arange(n), HEAD_DIM)
    for l in range(DEPTH):
        proj = x @ w_in[l]
        qa, ka, va, qb, kb, vb = jnp.split(proj, IN_SPLITS, axis=-1)
        qa = apply_rope(qa.reshape(b, n, N_HEADS_A, HEAD_DIM), cos, sin)
        ka = apply_rope(ka.reshape(b, n, N_HEADS_A, HEAD_DIM), cos, sin)
        va = va.reshape(b, n, N_HEADS_A, HEAD_DIM)
        outs, lses = [], []
        for window, dilation in DILATED_PATTERNS:
            o_i, lse_i = dilated_window_attention(qa, ka, va, window, dilation)
            outs.append(o_i)
            lses.append(lse_i)
        wts = jax.nn.softmax(jnp.stack(lses), axis=0)
        oa = jnp.einsum('pbnh,pbnhe->bnhe', wts, jnp.stack(outs).astype(jnp.float32)).astype(x.dtype)
        oa = rms_norm(oa.reshape(b, n, WIDTH_A), out_norm_a[l])

        qb = apply_axial_rope(rms_norm(qb.reshape(b, n, N_HEADS_B, HEAD_DIM), q_norm_g[l]))
        kb = apply_axial_rope(rms_norm(kb.reshape(b, n, N_KV_B, HEAD_DIM), k_norm_g[l]))
        vb = vb.reshape(b, n, N_KV_B, HEAD_DIM)
        ob = rms_norm(gqa_blocked(qb, kb, vb).reshape(b, n, WIDTH_B), out_norm_b[l])

        mix = jnp.concatenate([oa, ob], axis=-1) @ w_o[l]
        x = layer_norm(alpha * x + mix, ln1_g[l], ln1_b[l])
        x = layer_norm(alpha * x + memory_cross_attention(x, mem, wc_q[l], wc_k[l], wc_v[l], wc_o[l]), ln2_g[l], ln2_b[l])
        x = layer_norm(alpha * x + conv_glu_ffn(x, w_up[l], conv_w[l], conv_b[l], w_down[l]), ln3_g[l], ln3_b[l])
    return x


def setup_inputs(seed: int = 0) -> dict:
    key = jax.random.key(seed)
    ks = jax.random.split(key, 28)
    f32 = jnp.float32
    beta = (8 * DEPTH) ** -0.25

    def normal(k, shape, scale):
        return jax.random.normal(k, shape, f32) * scale

    def gain(k, shape):
        return 1.0 + 0.02 * jax.random.normal(k, shape, f32)

    return {
        'x_prompt': normal(ks[0], (BATCH, SEQ, D_MODEL), 1.0),
        'x_sample': normal(ks[1], (DEC_BATCH, DEC_SEQ, D_MODEL), 1.0),
        'mem_prompt': normal(ks[2], (BATCH, N_MEM, D_MODEL), 1.0),
        'mem_sample': normal(ks[3], (DEC_BATCH, N_MEM, D_MODEL), 1.0),
        'w_in': normal(ks[4], (DEPTH, D_MODEL, IN_COLS), D_MODEL ** -0.5),
        'q_norm_g': gain(ks[5], (DEPTH, HEAD_DIM)),
        'k_norm_g': gain(ks[6], (DEPTH, HEAD_DIM)),
        'out_norm_a': gain(ks[7], (DEPTH, WIDTH_A)),
        'out_norm_b': gain(ks[8], (DEPTH, WIDTH_B)),
        'w_o': normal(ks[9], (DEPTH, MIX_WIDTH, D_MODEL), beta * MIX_WIDTH ** -0.5),
        'ln1_g': gain(ks[10], (DEPTH, D_MODEL)),
        'ln1_b': normal(ks[11], (DEPTH, D_MODEL), 0.02),
        'wc_q': normal(ks[12], (DEPTH, D_MODEL, D_MODEL), D_MODEL ** -0.5),
        'wc_k': normal(ks[13], (DEPTH, D_MODEL, D_MODEL), D_MODEL ** -0.5),
        'wc_v': normal(ks[14], (DEPTH, D_MODEL, D_MODEL), D_MODEL ** -0.5),
        'wc_o': normal(ks[15], (DEPTH, D_MODEL, D_MODEL), beta * D_MODEL ** -0.5),
        'ln2_g': gain(ks[16], (DEPTH, D_MODEL)),
        'ln2_b': normal(ks[17], (DEPTH, D_MODEL), 0.02),
        'w_up': normal(ks[18], (DEPTH, D_MODEL, 2 * D_FF), D_MODEL ** -0.5),
        'conv_w': normal(ks[19], (DEPTH, CONV_W, D_FF), CONV_W ** -0.5),
        'conv_b': normal(ks[20], (DEPTH, D_FF), 0.02),
        'w_down': normal(ks[21], (DEPTH, D_FF, D_MODEL), beta * D_FF ** -0.5),
        'ln3_g': gain(ks[22], (DEPTH, D_MODEL)),
        'ln3_b': normal(ks[23], (DEPTH, D_MODEL), 0.02),
    }


def reference(x_prompt, x_sample, mem_prompt, mem_sample, w_in, q_norm_g, k_norm_g, out_norm_a, out_norm_b,
              w_o, ln1_g, ln1_b, wc_q, wc_k, wc_v, wc_o, ln2_g, ln2_b, w_up, conv_w, conv_b, w_down, ln3_g, ln3_b):
    y_prompt = encoder(x_prompt, mem_prompt, w_in, q_norm_g, k_norm_g, out_norm_a, out_norm_b, w_o, ln1_g, ln1_b,
                       wc_q, wc_k, wc_v, wc_o, ln2_g, ln2_b, w_up, conv_w, conv_b, w_down, ln3_g, ln3_b)
    y_sample = encoder(x_sample, mem_sample, w_in, q_norm_g, k_norm_g, out_norm_a, out_norm_b, w_o, ln1_g, ln1_b,
                       wc_q, wc_k, wc_v, wc_o, ln2_g, ln2_b, w_up, conv_w, conv_b, w_down, ln3_g, ln3_b)
    return (y_prompt, y_sample)
```

```python
import functools

import jax
import jax.numpy as jnp
from jax import lax
from jax.experimental import pallas as pl
from jax.experimental.pallas import tpu as pltpu

D_MODEL = 1024
HEAD_DIM = 64
N_HEADS_A = 8
N_HEADS_B = 8
N_KV_B = 2
WIDTH_A = N_HEADS_A * HEAD_DIM
WIDTH_B = N_HEADS_B * HEAD_DIM
KV_WIDTH_B = N_KV_B * HEAD_DIM
DILATED_PATTERNS = ((128, 1), (512, 4), (2048, 16))
N_HEADS_MEM = 4
HEAD_DIM_MEM = D_MODEL // N_HEADS_MEM
D_FF = 2816
GRID_W = 64
ROPE_THETA = 10000.0
LN_EPS = 1e-5
RMS_EPS = 1e-6

LANES = 128
SUBLANES = 8
VMEM_LIMIT_BYTES = 56 * 1024 * 1024

ROW_TILE = 512
BAND_Q = 256
GQA_Q = 256
GQA_K = 512
FF_CHUNK = 256

F32 = jnp.float32
BF16 = jnp.bfloat16


def _params(*semantics):
    return pltpu.CompilerParams(dimension_semantics=semantics, vmem_limit_bytes=VMEM_LIMIT_BYTES)


def _const_spec(shape):
    zeros = (0,) * len(shape)
    return pl.BlockSpec(shape, lambda *_: zeros, pipeline_mode=pl.Buffered(1))


def _swap_within(t, half, first):
    n = t.shape[-1]
    fwd = pltpu.roll(t, n - half, axis=1)
    bwd = pltpu.roll(t, half, axis=1)
    return jnp.where(first, fwd, bwd)


def _dot(a, b):
    return jnp.dot(a, b, preferred_element_type=F32)


def _dot_nt(a, b):
    return lax.dot_general(a, b, (((1,), (1,)), ((), ())), preferred_element_type=F32)


def _proj_kernel(x_ref, w_ref, ca_ref, sa_ref, cb_ref, sb_ref, gq_ref, gk_ref, gsum_ref,
                 qa_ref, ka_ref, va_ref, qb_ref, kb_ref, vb_ref):
    rows = x_ref.shape[0]
    xb = x_ref[...].astype(BF16)
    lane = lax.broadcasted_iota(jnp.int32, (rows, LANES), 1)
    first32 = (lane % 64) < 32
    first16 = (lane % 32) < 16
    low = lane < 64
    ca, sa = ca_ref[...], sa_ref[...]
    cb, sb = cb_ref[...], sb_ref[...]
    gsum = gsum_ref[...]

    def rope_a(t, scale):
        return (t * ca + _swap_within(t, 32, first32) * sa) * scale

    def head_rms(t, g):
        sq = t * t
        hi = sq.astype(BF16)
        lo = (sq - hi.astype(F32)).astype(BF16)
        ss = _dot(hi, gsum) + _dot(lo, gsum)
        return t * lax.rsqrt(ss * (1.0 / HEAD_DIM) + RMS_EPS) * g

    def rope_b(t, scale):
        return (t * cb + _swap_within(t, 16, first16) * sb) * scale

    qscale = HEAD_DIM ** -0.5
    c0 = 0
    qa = _dot(xb, w_ref[:, c0:c0 + WIDTH_A]); c0 += WIDTH_A
    for c in range(WIDTH_A // LANES):
        sl = slice(c * LANES, (c + 1) * LANES)
        qa_ref[:, sl] = rope_a(qa[:, sl], qscale).astype(BF16)
    ka = _dot(xb, w_ref[:, c0:c0 + WIDTH_A]); c0 += WIDTH_A
    for c in range(WIDTH_A // LANES):
        sl = slice(c * LANES, (c + 1) * LANES)
        ka_ref[:, sl] = rope_a(ka[:, sl], 1.0).astype(BF16)
    va_ref[...] = _dot(xb, w_ref[:, c0:c0 + WIDTH_A]).astype(BF16); c0 += WIDTH_A

    qb = _dot(xb, w_ref[:, c0:c0 + WIDTH_B]); c0 += WIDTH_B
    gq = gq_ref[...]
    zero = jnp.zeros((rows, LANES), F32)
    for c in range(WIDTH_B // LANES):
        y = rope_b(head_rms(qb[:, c * LANES:(c + 1) * LANES], gq), qscale)
        rolled = pltpu.roll(y, 64, axis=1)
        if (2 * c) // (N_HEADS_B // N_KV_B) == 0:
            even, odd = jnp.where(low, y, zero), jnp.where(low, rolled, zero)
        else:
            even, odd = jnp.where(low, zero, rolled), jnp.where(low, zero, y)
        qb_ref[:, (2 * c) * LANES:(2 * c + 1) * LANES] = even.astype(BF16)
        qb_ref[:, (2 * c + 1) * LANES:(2 * c + 2) * LANES] = odd.astype(BF16)

    kv = _dot(xb, w_ref[:, c0:c0 + 2 * KV_WIDTH_B])
    kb_ref[...] = rope_b(head_rms(kv[:, :KV_WIDTH_B], gk_ref[...]), 1.0).astype(BF16)
    vb_ref[...] = kv[:, KV_WIDTH_B:].astype(BF16)


def _rope_tables(n):
    pos = jnp.arange(n)

    def angles(p, dim):
        inv = ROPE_THETA ** (-jnp.arange(0, dim, 2, dtype=F32) / dim)
        ang = p.astype(F32)[:, None] * inv[None, :]
        return jnp.cos(ang), jnp.sin(ang)

    c, s = angles(pos, HEAD_DIM)
    ca = jnp.tile(jnp.concatenate([c, c], axis=-1), (1, 2))
    sa = jnp.tile(jnp.concatenate([-s, s], axis=-1), (1, 2))
    cr, sr = angles(pos // GRID_W, HEAD_DIM // 2)
    cc, sc = angles(pos % GRID_W, HEAD_DIM // 2)
    cb = jnp.tile(jnp.concatenate([cr, cr, cc, cc], axis=-1), (1, 2))
    sb = jnp.tile(jnp.concatenate([-sr, sr, -sc, sc], axis=-1), (1, 2))
    return ca, sa, cb, sb


def _project(x, w_in, q_norm_g, k_norm_g):
    b, n, _ = x.shape
    tm = min(ROW_TILE, n)
    ca, sa, cb, sb = _rope_tables(n)
    gq = jnp.tile(q_norm_g, 2)[None, :]
    gk = jnp.tile(k_norm_g, 2)[None, :]
    grp = jnp.arange(LANES) // HEAD_DIM
    gsum = (grp[:, None] == grp[None, :]).astype(BF16)
    in_cols = w_in.shape[1]
    tab = pl.BlockSpec((tm, LANES), lambda bi, i: (i, 0))

    def out(width):
        return pl.BlockSpec((None, tm, width), lambda bi, i: (bi, i, 0))

    return pl.pallas_call(
        _proj_kernel,
        grid=(b, n // tm),
        in_specs=[pl.BlockSpec((None, tm, D_MODEL), lambda bi, i: (bi, i, 0)),
                  _const_spec((D_MODEL, in_cols)), tab, tab, tab, tab,
                  _const_spec((1, LANES)), _const_spec((1, LANES)), _const_spec((LANES, LANES))],
        out_specs=[out(WIDTH_A), out(WIDTH_A), out(WIDTH_A), out(2 * WIDTH_B),
                   out(KV_WIDTH_B), out(KV_WIDTH_B)],
        out_shape=[jax.ShapeDtypeStruct((b, n, WIDTH_A), BF16)] * 3
        + [jax.ShapeDtypeStruct((b, n, 2 * WIDTH_B), BF16)]
        + [jax.ShapeDtypeStruct((b, n, KV_WIDTH_B), BF16)] * 2,
        compiler_params=_params("parallel", "parallel"),
        name="proj",
    )(x, w_in.astype(BF16), ca, sa, cb, sb, gq, gk, gsum)


def _band_kernel(q_ref, k_ref, v_ref, o_ref, lse_ref, *, half, tq, win):
    length = q_ref.shape[0]
    lane = lax.broadcasted_iota(jnp.int32, (tq, LANES), 1)
    low = lane < 64
    qi = lax.broadcasted_iota(jnp.int32, (tq, win), 0)
    kj = lax.broadcasted_iota(jnp.int32, (tq, win), 1)

    def chunk(i, carry):
        q0 = pl.multiple_of(i * tq, tq)
        k0 = pl.multiple_of(jnp.clip(q0 - half, 0, length - win), half)
        q = q_ref[pl.ds(q0, tq), :]
        k = k_ref[pl.ds(k0, win), :]
        v = v_ref[pl.ds(k0, win), :]
        valid = jnp.abs((kj + k0) - (qi + q0)) <= half
        outs, lses = [], []
        for hmask in (low, jnp.logical_not(low)):
            qm = jnp.where(hmask, q, jnp.zeros_like(q))
            s = jnp.where(valid, _dot_nt(qm, k), -jnp.inf)
            m = jnp.max(s, axis=-1, keepdims=True)
            p = jnp.exp(s - m)
            den = jnp.sum(p, axis=-1, keepdims=True)
            outs.append(_dot(p.astype(BF16), v) / den)
            lses.append(jnp.broadcast_to(m + jnp.log(den), (tq, LANES)))
        o_ref[pl.ds(q0, tq), :] = jnp.where(low, outs[0], outs[1]).astype(o_ref.dtype)
        lse_ref[pl.ds(q0, tq), :] = jnp.where(low, lses[0], lses[1])
        return carry

    lax.fori_loop(0, length // tq, chunk, 0)


def _banded_attention(qa, ka, va, window, dilation):
    b, n, width = qa.shape
    d = dilation
    half = window // (2 * d)
    length = n // d
    tq = min(BAND_Q, length)
    win = min(tq + 2 * half, length)
    blocks = width // LANES
    view = lambda t: t.reshape(b, length, d * width)
    spec = pl.BlockSpec((None, length, LANES), lambda bi, r, h: (bi, 0, r * blocks + h))
    o, lse = pl.pallas_call(
        functools.partial(_band_kernel, half=half, tq=tq, win=win),
        grid=(b, d, blocks),
        in_specs=[spec, spec, spec],
        out_specs=[spec, spec],
        out_shape=[jax.ShapeDtypeStruct((b, length, d * width), BF16),
                   jax.ShapeDtypeStruct((b, length, d * width), F32)],
        compiler_params=_params("parallel", "parallel", "parallel"),
        name=f"band_d{d}",
    )(view(qa), view(ka), view(va))
    return o.reshape(b, n, width), lse.reshape(b, n, width)


def _gqa_kernel(q_ref, k_ref, v_ref, o_ref, *, tk):
    tq = q_ref.shape[0]
    n = k_ref.shape[0]
    q2 = jnp.concatenate([q_ref[:, :LANES], q_ref[:, LANES:]], axis=0)

    def step(c, carry):
        m, l, acc = carry
        k0 = pl.multiple_of(c * tk, tk)
        s = _dot_nt(q2, k_ref[pl.ds(k0, tk), :])
        m_new = jnp.maximum(m, jnp.max(s, axis=-1, keepdims=True))
        a = jnp.exp(m - m_new)
        p = jnp.exp(s - m_new)
        l = a * l + jnp.sum(p, axis=-1, keepdims=True)
        acc = a * acc + _dot(p.astype(BF16), v_ref[pl.ds(k0, tk), :])
        return m_new, l, acc

    init = (jnp.full((2 * tq, 1), -jnp.inf, F32), jnp.zeros((2 * tq, 1), F32),
            jnp.zeros((2 * tq, LANES), F32))
    _, l, acc = lax.fori_loop(0, n // tk, step, init)
    o = acc / l
    even, odd = o[:tq], o[tq:]
    kv0 = pl.program_id(2) < (N_HEADS_B // N_KV_B) // 2
    left = jnp.where(kv0, even, pltpu.roll(even, 64, axis=1))
    right = jnp.where(kv0, pltpu.roll(odd, 64, axis=1), odd)
    lane = lax.broadcasted_iota(jnp.int32, (tq, LANES), 1)
    o_ref[...] = jnp.where(lane < 64, left, right).astype(o_ref.dtype)


def _gqa_attention(qb, kb, vb):
    b, n, _ = qb.shape
    tq = min(GQA_Q, n)
    tk = min(GQA_K, n)
    kvspec = pl.BlockSpec((None, n, KV_WIDTH_B), lambda bi, i, j: (bi, 0, 0))
    return pl.pallas_call(
        functools.partial(_gqa_kernel, tk=tk),
        grid=(b, n // tq, WIDTH_B // LANES),
        in_specs=[pl.BlockSpec((None, tq, 2 * LANES), lambda bi, i, j: (bi, i, j)), kvspec, kvspec],
        out_specs=pl.BlockSpec((None, tq, LANES), lambda bi, i, j: (bi, i, j)),
        out_shape=jax.ShapeDtypeStruct((b, n, WIDTH_B), BF16),
        compiler_params=_params("parallel", "parallel", "parallel"),
        name="gqa",
    )(qb, kb, vb)


def _memkv_kernel(mem_ref, wk_ref, wv_ref, k_ref, v_ref):
    mb = mem_ref[...].astype(BF16)
    k_ref[...] = _dot(mb, wk_ref[...]).astype(BF16)
    v_ref[...] = _dot(mb, wv_ref[...]).astype(BF16)


def _memory_kv(mem, wk, wv):
    b, m, _ = mem.shape
    spec = pl.BlockSpec((None, m, D_MODEL), lambda bi: (bi, 0, 0))
    wspec = _const_spec((D_MODEL, D_MODEL))
    return pl.pallas_call(
        _memkv_kernel,
        grid=(b,),
        in_specs=[spec, wspec, wspec],
        out_specs=[spec, spec],
        out_shape=[jax.ShapeDtypeStruct((b, m, D_MODEL), BF16)] * 2,
        compiler_params=_params("parallel"),
        name="memkv",
    )(mem, wk.astype(BF16), wv.astype(BF16))


def _layer_norm(x, g, b):
    mu = jnp.mean(x, axis=-1, keepdims=True)
    xc = x - mu
    var = jnp.mean(xc * xc, axis=-1, keepdims=True)
    return xc * lax.rsqrt(var + LN_EPS) * g + b


def _rms_norm(x, g):
    ms = jnp.mean(x * x, axis=-1, keepdims=True)
    return x * lax.rsqrt(ms + RMS_EPS) * g


def _mix_kernel(x_ref, o1_ref, o2_ref, o3_ref, l1_ref, l2_ref, l3_ref, ob_ref,
                ga_ref, gb_ref, wo_ref, g1_ref, b1_ref, wq_ref, km_ref, vm_ref, wco_ref,
                g2_ref, b2_ref, y_ref, *, alpha):
    l1, l2, l3 = l1_ref[...], l2_ref[...], l3_ref[...]
    mx = jnp.maximum(jnp.maximum(l1, l2), l3)
    e1, e2, e3 = jnp.exp(l1 - mx), jnp.exp(l2 - mx), jnp.exp(l3 - mx)
    oa = (e1 * o1_ref[...].astype(F32) + e2 * o2_ref[...].astype(F32)
          + e3 * o3_ref[...].astype(F32)) / (e1 + e2 + e3)
    oa = _rms_norm(oa, ga_ref[...]).astype(BF16)
    ob = _rms_norm(ob_ref[...].astype(F32), gb_ref[...]).astype(BF16)
    mix = _dot(oa, wo_ref[:WIDTH_A, :]) + _dot(ob, wo_ref[WIDTH_A:, :])
    x1 = _layer_norm(alpha * x_ref[...] + mix, g1_ref[...], b1_ref[...])

    q = (_dot(x1.astype(BF16), wq_ref[...]) * (HEAD_DIM_MEM ** -0.5)).astype(BF16)
    heads = []
    for h in range(N_HEADS_MEM):
        sl = slice(h * HEAD_DIM_MEM, (h + 1) * HEAD_DIM_MEM)
        s = _dot_nt(q[:, sl], km_ref[:, sl])
        e = jnp.exp(s - jnp.max(s, axis=-1, keepdims=True))
        den = jnp.sum(e, axis=-1, keepdims=True)
        heads.append((_dot(e.astype(BF16), vm_ref[:, sl]) / den).astype(BF16))
    cross = _dot(jnp.concatenate(heads, axis=-1), wco_ref[...])
    y_ref[...] = _layer_norm(alpha * x1 + cross, g2_ref[...], b2_ref[...])


def _mix_cross(x, outs, lses, ob, kmem, vmem, out_norm_a, out_norm_b, w_o, ln1_g, ln1_b,
               wc_q, wc_o, ln2_g, ln2_b, alpha):
    b, n, _ = x.shape
    tm = min(ROW_TILE, n)
    m = kmem.shape[1]

    def rows(width):
        return pl.BlockSpec((None, tm, width), lambda bi, i: (bi, i, 0))

    vec = lambda width: _const_spec((1, width))
    memspec = pl.BlockSpec((None, m, D_MODEL), lambda bi, i: (bi, 0, 0))
    wspec = _const_spec((D_MODEL, D_MODEL))
    return pl.pallas_call(
        functools.partial(_mix_kernel, alpha=alpha),
        grid=(b, n // tm),
        in_specs=[rows(D_MODEL)] + [rows(WIDTH_A)] * 6 + [rows(WIDTH_B), vec(WIDTH_A), vec(WIDTH_B),
                  wspec, vec(D_MODEL), vec(D_MODEL), wspec, memspec, memspec, wspec,
                  vec(D_MODEL), vec(D_MODEL)],
        out_specs=rows(D_MODEL),
        out_shape=jax.ShapeDtypeStruct((b, n, D_MODEL), F32),
        compiler_params=_params("parallel", "parallel"),
        name="mix",
    )(x, *outs, *lses, ob, out_norm_a[None, :], out_norm_b[None, :], w_o.astype(BF16),
      ln1_g[None, :], ln1_b[None, :], wc_q.astype(BF16), kmem, vmem, wc_o.astype(BF16),
      ln2_g[None, :], ln2_b[None, :])


def _ffn_kernel(x_ref, prev_ref, next_ref, wup_ref, cw_ref, cb_ref, wdn_ref, g_ref, b_ref,
                y_ref, h_ref, *, alpha):
    rows = x_ref.shape[0]
    i = pl.program_id(1)
    x = x_ref[...]
    prev = jnp.where(i > 0, prev_ref[...], 0.0)
    nxt = jnp.where(i < pl.num_programs(1) - 1, next_ref[...], 0.0)
    xb = x.astype(BF16)
    xe = jnp.concatenate([prev, x, nxt], axis=0).astype(BF16)
    ext = rows + 2 * SUBLANES
    for c in range(D_FF // FF_CHUNK):
        sl = slice(c * FF_CHUNK, (c + 1) * FF_CHUNK)
        gate = _dot(xe, wup_ref[:, sl])
        before = pltpu.roll(gate, 1, axis=0)[SUBLANES:SUBLANES + rows]
        after = pltpu.roll(gate, ext - 1, axis=0)[SUBLANES:SUBLANES + rows]
        here = gate[SUBLANES:SUBLANES + rows]
        conv = (before * cw_ref[0:1, sl] + here * cw_ref[1:2, sl] + after * cw_ref[2:3, sl]
                + cb_ref[:, sl])
        act = 0.5 * conv * (1.0 + lax.erf(conv * (2.0 ** -0.5)))
        val = _dot(xb, wup_ref[:, D_FF + c * FF_CHUNK:D_FF + (c + 1) * FF_CHUNK])
        h_ref[:, sl] = (act * val).astype(BF16)
    y = _dot(h_ref[...], wdn_ref[...])
    y_ref[...] = _layer_norm(alpha * x + y, g_ref[...], b_ref[...])


def _conv_glu_ffn(x, w_up, conv_w, conv_b, w_down, ln_g, ln_b, alpha):
    b, n, _ = x.shape
    tm = min(ROW_TILE, n)
    per = tm // SUBLANES
    last = n // SUBLANES - 1
    vec = _const_spec((1, D_MODEL))
    return pl.pallas_call(
        functools.partial(_ffn_kernel, alpha=alpha),
        grid=(b, n // tm),
        in_specs=[pl.BlockSpec((None, tm, D_MODEL), lambda bi, i: (bi, i, 0)),
                  pl.BlockSpec((None, SUBLANES, D_MODEL),
                               lambda bi, i: (bi, jnp.maximum(i * per - 1, 0), 0)),
                  pl.BlockSpec((None, SUBLANES, D_MODEL),
                               lambda bi, i: (bi, jnp.minimum((i + 1) * per, last), 0)),
                  _const_spec((D_MODEL, 2 * D_FF)), _const_spec((3, D_FF)), _const_spec((1, D_FF)),
                  _const_spec((D_FF, D_MODEL)), vec, vec],
        out_specs=pl.BlockSpec((None, tm, D_MODEL), lambda bi, i: (bi, i, 0)),
        out_shape=jax.ShapeDtypeStruct((b, n, D_MODEL), F32),
        scratch_shapes=[pltpu.VMEM((tm, D_FF), BF16)],
        compiler_params=_params("parallel", "parallel"),
        name="ffn",
    )(x, x, x, w_up.astype(BF16), conv_w, conv_b[None, :], w_down.astype(BF16),
      ln_g[None, :], ln_b[None, :])


def _encoder(x, mem, w_in, q_norm_g, k_norm_g, out_norm_a, out_norm_b, w_o, ln1_g, ln1_b,
             wc_q, wc_k, wc_v, wc_o, ln2_g, ln2_b, w_up, conv_w, conv_b, w_down, ln3_g, ln3_b):
    depth = w_in.shape[0]
    alpha = (2 * depth) ** 0.25
    for l in range(depth):
        qa, ka, va, qb, kb, vb = _project(x, w_in[l], q_norm_g[l], k_norm_g[l])
        outs, lses = [], []
        for window, dilation in DILATED_PATTERNS:
            o_i, lse_i = _banded_attention(qa, ka, va, window, dilation)
            outs.append(o_i)
            lses.append(lse_i)
        ob = _gqa_attention(qb, kb, vb)
        kmem, vmem = _memory_kv(mem, wc_k[l], wc_v[l])
        x = _mix_cross(x, outs, lses, ob, kmem, vmem, out_norm_a[l], out_norm_b[l], w_o[l],
                       ln1_g[l], ln1_b[l], wc_q[l], wc_o[l], ln2_g[l], ln2_b[l], alpha)
        x = _conv_glu_ffn(x, w_up[l], conv_w[l], conv_b[l], w_down[l], ln3_g[l], ln3_b[l], alpha)
    return x


def kernel(x_prompt, x_sample, mem_prompt, mem_sample, w_in, q_norm_g, k_norm_g, out_norm_a, out_norm_b, w_o, ln1_g, ln1_b, wc_q, wc_k, wc_v, wc_o, ln2_g, ln2_b, w_up, conv_w, conv_b, w_down, ln3_g, ln3_b):
    weights = (w_in, q_norm_g, k_norm_g, out_norm_a, out_norm_b, w_o, ln1_g, ln1_b,
               wc_q, wc_k, wc_v, wc_o, ln2_g, ln2_b, w_up, conv_w, conv_b, w_down, ln3_g, ln3_b)
    return (_encoder(x_prompt, mem_prompt, *weights), _encoder(x_sample, mem_sample, *weights))
```

```python
import functools

import jax
import jax.numpy as jnp
from jax import lax
from jax.experimental import pallas as pl
from jax.experimental.pallas import tpu as pltpu

D_MODEL = 1024
HEAD_DIM = 64
N_HEADS_A = 8
N_HEADS_B = 8
N_KV_B = 2
WIDTH_A = N_HEADS_A * HEAD_DIM
WIDTH_B = N_HEADS_B * HEAD_DIM
KV_WIDTH_B = N_KV_B * HEAD_DIM
DILATED_PATTERNS = ((128, 1), (512, 4), (2048, 16))
N_HEADS_MEM = 4
HEAD_DIM_MEM = D_MODEL // N_HEADS_MEM
D_FF = 2816
GRID_W = 64
ROPE_THETA = 10000.0
LN_EPS = 1e-5
RMS_EPS = 1e-6
LOG2_E = 1.4426950408889634

LANES = 128
SUBLANES = 8
VMEM_LIMIT_BYTES = 56 * 1024 * 1024

ROW_TILE = 512
BAND_Q = 256
GQA_Q = 256
GQA_K = 512
FF_CHUNK = 256

F32 = jnp.float32
BF16 = jnp.bfloat16


def _params(*semantics):
    return pltpu.CompilerParams(dimension_semantics=semantics, vmem_limit_bytes=VMEM_LIMIT_BYTES)


def _const_spec(shape):
    zeros = (0,) * len(shape)
    return pl.BlockSpec(shape, lambda *_: zeros, pipeline_mode=pl.Buffered(1))


def _swap_within(t, half, first):
    n = t.shape[-1]
    fwd = pltpu.roll(t, n - half, axis=1)
    bwd = pltpu.roll(t, half, axis=1)
    return jnp.where(first, fwd, bwd)


def _dot(a, b):
    return jnp.dot(a, b, preferred_element_type=F32)


def _dot_nt(a, b):
    return lax.dot_general(a, b, (((1,), (1,)), ((), ())), preferred_element_type=F32)


def _proj_kernel(x_ref, w_ref, ca_ref, sa_ref, cb_ref, sb_ref, gq_ref, gk_ref, gsum_ref,
                 qa_ref, ka_ref, va_ref, qb_ref, kb_ref, vt_ref):
    rows = x_ref.shape[0]
    xb = x_ref[...].astype(BF16)
    lane = lax.broadcasted_iota(jnp.int32, (rows, LANES), 1)
    first32 = (lane % 64) < 32
    first16 = (lane % 32) < 16
    low = lane < 64
    ca, sa = ca_ref[...], sa_ref[...]
    cb, sb = cb_ref[...], sb_ref[...]
    gsum = gsum_ref[...]

    def rope_a(t, scale):
        return (t * ca + _swap_within(t, 32, first32) * sa) * scale

    def head_rms(t, g):
        sq = t * t
        hi = sq.astype(BF16)
        lo = (sq - hi.astype(F32)).astype(BF16)
        ss = _dot(hi, gsum) + _dot(lo, gsum)
        return t * lax.rsqrt(ss * (1.0 / HEAD_DIM) + RMS_EPS) * g

    def rope_b(t, scale):
        return (t * cb + _swap_within(t, 16, first16) * sb) * scale

    qscale = HEAD_DIM ** -0.5
    c0 = 0
    qa = _dot(xb, w_ref[:, c0:c0 + WIDTH_A]); c0 += WIDTH_A
    for c in range(WIDTH_A // LANES):
        sl = slice(c * LANES, (c + 1) * LANES)
        qa_ref[:, sl] = rope_a(qa[:, sl], qscale).astype(BF16)
    ka = _dot(xb, w_ref[:, c0:c0 + WIDTH_A]); c0 += WIDTH_A
    for c in range(WIDTH_A // LANES):
        sl = slice(c * LANES, (c + 1) * LANES)
        ka_ref[:, sl] = rope_a(ka[:, sl], 1.0).astype(BF16)
    va_ref[...] = _dot(xb, w_ref[:, c0:c0 + WIDTH_A]).astype(BF16); c0 += WIDTH_A

    qb = _dot(xb, w_ref[:, c0:c0 + WIDTH_B]); c0 += WIDTH_B
    gq = gq_ref[...]
    zero = jnp.zeros((rows, LANES), F32)
    for c in range(WIDTH_B // LANES):
        y = rope_b(head_rms(qb[:, c * LANES:(c + 1) * LANES], gq), qscale * LOG2_E)
        rolled = pltpu.roll(y, 64, axis=1)
        if (2 * c) // (N_HEADS_B // N_KV_B) == 0:
            even, odd = jnp.where(low, y, zero), jnp.where(low, rolled, zero)
        else:
            even, odd = jnp.where(low, zero, rolled), jnp.where(low, zero, y)
        qb_ref[:, (2 * c) * LANES:(2 * c + 1) * LANES] = even.astype(BF16)
        qb_ref[:, (2 * c + 1) * LANES:(2 * c + 2) * LANES] = odd.astype(BF16)

    kv = _dot(xb, w_ref[:, c0:c0 + 2 * KV_WIDTH_B])
    kb_ref[...] = rope_b(head_rms(kv[:, :KV_WIDTH_B], gk_ref[...]), 1.0).astype(BF16)
    vt = kv[:, KV_WIDTH_B:].T
    row = lax.broadcasted_iota(jnp.int32, vt.shape, 0)
    vt_ref[0] = jnp.where(row < HEAD_DIM, vt, 1.0).astype(BF16)
    vt_ref[1] = jnp.where(row < HEAD_DIM, 1.0, vt).astype(BF16)


def _rope_tables(n):
    pos = jnp.arange(n)

    def angles(p, dim):
        inv = ROPE_THETA ** (-jnp.arange(0, dim, 2, dtype=F32) / dim)
        ang = p.astype(F32)[:, None] * inv[None, :]
        return jnp.cos(ang), jnp.sin(ang)

    c, s = angles(pos, HEAD_DIM)
    ca = jnp.tile(jnp.concatenate([c, c], axis=-1), (1, 2))
    sa = jnp.tile(jnp.concatenate([-s, s], axis=-1), (1, 2))
    cr, sr = angles(pos // GRID_W, HEAD_DIM // 2)
    cc, sc = angles(pos % GRID_W, HEAD_DIM // 2)
    cb = jnp.tile(jnp.concatenate([cr, cr, cc, cc], axis=-1), (1, 2))
    sb = jnp.tile(jnp.concatenate([-sr, sr, -sc, sc], axis=-1), (1, 2))
    return ca, sa, cb, sb


def _project(x, w_in, q_norm_g, k_norm_g):
    b, n, _ = x.shape
    tm = min(ROW_TILE, n)
    ca, sa, cb, sb = _rope_tables(n)
    gq = jnp.tile(q_norm_g, 2)[None, :]
    gk = jnp.tile(k_norm_g, 2)[None, :]
    grp = jnp.arange(LANES) // HEAD_DIM
    gsum = (grp[:, None] == grp[None, :]).astype(BF16)
    in_cols = w_in.shape[1]
    tab = pl.BlockSpec((tm, LANES), lambda bi, i: (i, 0))

    def out(width):
        return pl.BlockSpec((None, tm, width), lambda bi, i: (bi, i, 0))

    return pl.pallas_call(
        _proj_kernel,
        grid=(b, n // tm),
        in_specs=[pl.BlockSpec((None, tm, D_MODEL), lambda bi, i: (bi, i, 0)),
                  _const_spec((D_MODEL, in_cols)), tab, tab, tab, tab,
                  _const_spec((1, LANES)), _const_spec((1, LANES)), _const_spec((LANES, LANES))],
        out_specs=[out(WIDTH_A), out(WIDTH_A), out(WIDTH_A), out(2 * WIDTH_B), out(KV_WIDTH_B),
                   pl.BlockSpec((None, N_KV_B, None, KV_WIDTH_B, tm), lambda bi, i: (bi, 0, i, 0, 0))],
        out_shape=[jax.ShapeDtypeStruct((b, n, WIDTH_A), BF16)] * 3
        + [jax.ShapeDtypeStruct((b, n, 2 * WIDTH_B), BF16),
           jax.ShapeDtypeStruct((b, n, KV_WIDTH_B), BF16),
           jax.ShapeDtypeStruct((b, N_KV_B, n // tm, KV_WIDTH_B, tm), BF16)],
        compiler_params=_params("parallel", "parallel"),
        name="proj",
    )(x, w_in.astype(BF16), ca, sa, cb, sb, gq, gk, gsum)


def _band_kernel(q_ref, k_ref, v_ref, o_ref, lse_ref, *, half, tq, win):
    length = q_ref.shape[0]
    lane = lax.broadcasted_iota(jnp.int32, (tq, LANES), 1)
    low = lane < 64
    qi = lax.broadcasted_iota(jnp.int32, (tq, win), 0)
    kj = lax.broadcasted_iota(jnp.int32, (tq, win), 1)

    def chunk(i, carry):
        q0 = pl.multiple_of(i * tq, tq)
        k0 = pl.multiple_of(jnp.clip(q0 - half, 0, length - win), half)
        q = q_ref[pl.ds(q0, tq), :]
        k = k_ref[pl.ds(k0, win), :]
        v = v_ref[pl.ds(k0, win), :]
        valid = jnp.abs((kj + k0) - (qi + q0)) <= half
        outs, lses = [], []
        for hmask in (low, jnp.logical_not(low)):
            qm = jnp.where(hmask, q, jnp.zeros_like(q))
            s = jnp.where(valid, _dot_nt(qm, k), -jnp.inf)
            m = jnp.max(s, axis=-1, keepdims=True)
            p = jnp.exp(s - m)
            den = jnp.sum(p, axis=-1, keepdims=True)
            outs.append(_dot(p.astype(BF16), v) / den)
            lses.append(jnp.broadcast_to(m + jnp.log(den), (tq, LANES)))
        o_ref[pl.ds(q0, tq), :] = jnp.where(low, outs[0], outs[1]).astype(o_ref.dtype)
        lse_ref[pl.ds(q0, tq), :] = jnp.where(low, lses[0], lses[1])
        return carry

    lax.fori_loop(0, length // tq, chunk, 0)


def _banded_attention(qa, ka, va, window, dilation):
    b, n, width = qa.shape
    d = dilation
    half = window // (2 * d)
    length = n // d
    tq = min(BAND_Q, length)
    win = min(tq + 2 * half, length)
    blocks = width // LANES
    view = lambda t: t.reshape(b, length, d * width)
    spec = pl.BlockSpec((None, length, LANES), lambda bi, r, h: (bi, 0, r * blocks + h))
    o, lse = pl.pallas_call(
        functools.partial(_band_kernel, half=half, tq=tq, win=win),
        grid=(b, d, blocks),
        in_specs=[spec, spec, spec],
        out_specs=[spec, spec],
        out_shape=[jax.ShapeDtypeStruct((b, length, d * width), BF16),
                   jax.ShapeDtypeStruct((b, length, d * width), F32)],
        compiler_params=_params("parallel", "parallel", "parallel"),
        name=f"band_d{d}",
    )(view(qa), view(ka), view(va))
    return o.reshape(b, n, width), lse.reshape(b, n, width)


def _gqa_kernel(q_ref, k_ref, vt_ref, o_ref, s_ref):
    tq = q_ref.shape[0]
    chunks, _, tk = vt_ref.shape
    qs = (q_ref[:, :LANES], q_ref[:, LANES:])

    def scores(c, slot):
        k = k_ref[pl.ds(pl.multiple_of(c * tk, tk), tk), :]
        tops = []
        for h, q in enumerate(qs):
            s = _dot_nt(k, q)
            s_ref[slot, h] = s
            tops.append(jnp.max(s, axis=0, keepdims=True))
        return tuple(tops)

    def accumulate(c, slot, tops, state):
        vt = vt_ref[c]
        new = []
        for h, (m, acc) in enumerate(state):
            m_new = jnp.maximum(m, tops[h])
            p = jnp.exp2(s_ref[slot, h] - m_new).astype(BF16)
            new.append((m_new, jnp.exp2(m - m_new) * acc + _dot(vt, p)))
        return tuple(new)

    def pair(c2, carry):
        tops0, state = carry
        c = 2 * c2
        tops1 = scores(c + 1, 1)
        state = accumulate(c, 0, tops0, state)
        tops2 = scores(c + 2, 0)
        state = accumulate(c + 1, 1, tops1, state)
        return tops2, state

    state = ((jnp.full((1, tq), -jnp.inf, F32), jnp.zeros((LANES, tq), F32)),) * 2
    tops0, state = lax.fori_loop(0, chunks // 2 - 1, pair, (scores(0, 0), state))
    tops1 = scores(chunks - 1, 1)
    state = accumulate(chunks - 2, 0, tops0, state)
    carry = accumulate(chunks - 1, 1, tops1, state)
    kv0 = pl.program_id(2) < (N_HEADS_B // N_KV_B) // 2
    outs = []
    for _, acc in carry:
        num = jnp.where(kv0, acc[:HEAD_DIM], acc[HEAD_DIM:])
        den = jnp.where(kv0, acc[HEAD_DIM:HEAD_DIM + 1], acc[0:1])
        outs.append(num / den)
    o_ref[...] = jnp.concatenate(outs, axis=0).T.astype(o_ref.dtype)


def _gqa_attention(qb, kb, vt):
    b, n, _ = qb.shape
    _, _, chunks, _, tk = vt.shape
    assert chunks >= 2 and chunks % 2 == 0, "the score pipeline handles key chunks in pairs"
    tq = min(GQA_Q, n)
    pairs_per_kv = (N_HEADS_B // N_KV_B) // 2
    return pl.pallas_call(
        _gqa_kernel,
        grid=(b, n // tq, WIDTH_B // LANES),
        in_specs=[pl.BlockSpec((None, tq, 2 * LANES), lambda bi, i, j: (bi, i, j)),
                  pl.BlockSpec((None, n, KV_WIDTH_B), lambda bi, i, j: (bi, 0, 0)),
                  pl.BlockSpec((None, None, chunks, KV_WIDTH_B, tk),
                               lambda bi, i, j: (bi, j // pairs_per_kv, 0, 0, 0))],
        out_specs=pl.BlockSpec((None, tq, LANES), lambda bi, i, j: (bi, i, j)),
        out_shape=jax.ShapeDtypeStruct((b, n, WIDTH_B), BF16),
        scratch_shapes=[pltpu.VMEM((2, 2, tk, tq), F32)],
        compiler_params=_params("parallel", "parallel", "parallel"),
        name="gqa",
    )(qb, kb, vt)


def _memkv_kernel(mem_ref, wk_ref, wv_ref, k_ref, v_ref):
    mb = mem_ref[...].astype(BF16)
    k_ref[...] = _dot(mb, wk_ref[...]).astype(BF16)
    v_ref[...] = _dot(mb, wv_ref[...]).astype(BF16)


def _memory_kv(mem, wk, wv):
    b, m, _ = mem.shape
    spec = pl.BlockSpec((None, m, D_MODEL), lambda bi: (bi, 0, 0))
    wspec = _const_spec((D_MODEL, D_MODEL))
    return pl.pallas_call(
        _memkv_kernel,
        grid=(b,),
        in_specs=[spec, wspec, wspec],
        out_specs=[spec, spec],
        out_shape=[jax.ShapeDtypeStruct((b, m, D_MODEL), BF16)] * 2,
        compiler_params=_params("parallel"),
        name="memkv",
    )(mem, wk.astype(BF16), wv.astype(BF16))


def _layer_norm(x, g, b):
    mu = jnp.mean(x, axis=-1, keepdims=True)
    xc = x - mu
    var = jnp.mean(xc * xc, axis=-1, keepdims=True)
    return xc * lax.rsqrt(var + LN_EPS) * g + b


def _rms_norm(x, g):
    ms = jnp.mean(x * x, axis=-1, keepdims=True)
    return x * lax.rsqrt(ms + RMS_EPS) * g


def _mix_kernel(x_ref, o1_ref, o2_ref, o3_ref, l1_ref, l2_ref, l3_ref, ob_ref,
                ga_ref, gb_ref, wo_ref, g1_ref, b1_ref, wq_ref, km_ref, vm_ref, wco_ref,
                g2_ref, b2_ref, y_ref, *, alpha):
    l1, l2, l3 = l1_ref[...], l2_ref[...], l3_ref[...]
    mx = jnp.maximum(jnp.maximum(l1, l2), l3)
    e1, e2, e3 = jnp.exp(l1 - mx), jnp.exp(l2 - mx), jnp.exp(l3 - mx)
    oa = (e1 * o1_ref[...].astype(F32) + e2 * o2_ref[...].astype(F32)
          + e3 * o3_ref[...].astype(F32)) / (e1 + e2 + e3)
    oa = _rms_norm(oa, ga_ref[...]).astype(BF16)
    ob = _rms_norm(ob_ref[...].astype(F32), gb_ref[...]).astype(BF16)
    mix = _dot(oa, wo_ref[:WIDTH_A, :]) + _dot(ob, wo_ref[WIDTH_A:, :])
    x1 = _layer_norm(alpha * x_ref[...] + mix, g1_ref[...], b1_ref[...])

    q = (_dot(x1.astype(BF16), wq_ref[...]) * (HEAD_DIM_MEM ** -0.5)).astype(BF16)
    heads = []
    for h in range(N_HEADS_MEM):
        sl = slice(h * HEAD_DIM_MEM, (h + 1) * HEAD_DIM_MEM)
        s = _dot_nt(q[:, sl], km_ref[:, sl])
        e = jnp.exp(s - jnp.max(s, axis=-1, keepdims=True))
        den = jnp.sum(e, axis=-1, keepdims=True)
        heads.append((_dot(e.astype(BF16), vm_ref[:, sl]) / den).astype(BF16))
    cross = _dot(jnp.concatenate(heads, axis=-1), wco_ref[...])
    y_ref[...] = _layer_norm(alpha * x1 + cross, g2_ref[...], b2_ref[...])


def _mix_cross(x, outs, lses, ob, kmem, vmem, out_norm_a, out_norm_b, w_o, ln1_g, ln1_b,
               wc_q, wc_o, ln2_g, ln2_b, alpha):
    b, n, _ = x.shape
    tm = min(ROW_TILE, n)
    m = kmem.shape[1]

    def rows(width):
        return pl.BlockSpec((None, tm, width), lambda bi, i: (bi, i, 0))

    vec = lambda width: _const_spec((1, width))
    memspec = pl.BlockSpec((None, m, D_MODEL), lambda bi, i: (bi, 0, 0))
    wspec = _const_spec((D_MODEL, D_MODEL))
    return pl.pallas_call(
        functools.partial(_mix_kernel, alpha=alpha),
        grid=(b, n // tm),
        in_specs=[rows(D_MODEL)] + [rows(WIDTH_A)] * 6 + [rows(WIDTH_B), vec(WIDTH_A), vec(WIDTH_B),
                  wspec, vec(D_MODEL), vec(D_MODEL), wspec, memspec, memspec, wspec,
                  vec(D_MODEL), vec(D_MODEL)],
        out_specs=rows(D_MODEL),
        out_shape=jax.ShapeDtypeStruct((b, n, D_MODEL), F32),
        compiler_params=_params("parallel", "parallel"),
        name="mix",
    )(x, *outs, *lses, ob, out_norm_a[None, :], out_norm_b[None, :], w_o.astype(BF16),
      ln1_g[None, :], ln1_b[None, :], wc_q.astype(BF16), kmem, vmem, wc_o.astype(BF16),
      ln2_g[None, :], ln2_b[None, :])


def _ffn_kernel(x_ref, prev_ref, next_ref, wup_ref, cw_ref, cb_ref, wdn_ref, g_ref, b_ref,
                y_ref, h_ref, *, alpha):
    rows = x_ref.shape[0]
    i = pl.program_id(1)
    x = x_ref[...]
    prev = jnp.where(i > 0, prev_ref[...], 0.0)
    nxt = jnp.where(i < pl.num_programs(1) - 1, next_ref[...], 0.0)
    xb = x.astype(BF16)
    xe = jnp.concatenate([prev, x, nxt], axis=0).astype(BF16)
    ext = rows + 2 * SUBLANES
    for c in range(D_FF // FF_CHUNK):
        sl = slice(c * FF_CHUNK, (c + 1) * FF_CHUNK)
        gate = _dot(xe, wup_ref[:, sl])
        before = pltpu.roll(gate, 1, axis=0)[SUBLANES:SUBLANES + rows]
        after = pltpu.roll(gate, ext - 1, axis=0)[SUBLANES:SUBLANES + rows]
        here = gate[SUBLANES:SUBLANES + rows]
        conv = (before * cw_ref[0:1, sl] + here * cw_ref[1:2, sl] + after * cw_ref[2:3, sl]
                + cb_ref[:, sl])
        act = 0.5 * conv * (1.0 + lax.erf(conv * (2.0 ** -0.5)))
        val = _dot(xb, wup_ref[:, D_FF + c * FF_CHUNK:D_FF + (c + 1) * FF_CHUNK])
        h_ref[:, sl] = (act * val).astype(BF16)
    y = _dot(h_ref[...], wdn_ref[...])
    y_ref[...] = _layer_norm(alpha * x + y, g_ref[...], b_ref[...])


def _conv_glu_ffn(x, w_up, conv_w, conv_b, w_down, ln_g, ln_b, alpha):
    b, n, _ = x.shape
    tm = min(ROW_TILE, n)
    per = tm // SUBLANES
    last = n // SUBLANES - 1
    vec = _const_spec((1, D_MODEL))
    return pl.pallas_call(
        functools.partial(_ffn_kernel, alpha=alpha),
        grid=(b, n // tm),
        in_specs=[pl.BlockSpec((None, tm, D_MODEL), lambda bi, i: (bi, i, 0)),
                  pl.BlockSpec((None, SUBLANES, D_MODEL),
                               lambda bi, i: (bi, jnp.maximum(i * per - 1, 0), 0)),
                  pl.BlockSpec((None, SUBLANES, D_MODEL),
                               lambda bi, i: (bi, jnp.minimum((i + 1) * per, last), 0)),
                  _const_spec((D_MODEL, 2 * D_FF)), _const_spec((3, D_FF)), _const_spec((1, D_FF)),
                  _const_spec((D_FF, D_MODEL)), vec, vec],
        out_specs=pl.BlockSpec((None, tm, D_MODEL), lambda bi, i: (bi, i, 0)),
        out_shape=jax.ShapeDtypeStruct((b, n, D_MODEL), F32),
        scratch_shapes=[pltpu.VMEM((tm, D_FF), BF16)],
        compiler_params=_params("parallel", "parallel"),
        name="ffn",
    )(x, x, x, w_up.astype(BF16), conv_w, conv_b[None, :], w_down.astype(BF16),
      ln_g[None, :], ln_b[None, :])


def _encoder(x, mem, w_in, q_norm_g, k_norm_g, out_norm_a, out_norm_b, w_o, ln1_g, ln1_b,
             wc_q, wc_k, wc_v, wc_o, ln2_g, ln2_b, w_up, conv_w, conv_b, w_down, ln3_g, ln3_b):
    depth = w_in.shape[0]
    alpha = (2 * depth) ** 0.25
    for l in range(depth):
        qa, ka, va, qb, kb, vb = _project(x, w_in[l], q_norm_g[l], k_norm_g[l])
        outs, lses = [], []
        for window, dilation in DILATED_PATTERNS:
            o_i, lse_i = _banded_attention(qa, ka, va, window, dilation)
            outs.append(o_i)
            lses.append(lse_i)
        ob = _gqa_attention(qb, kb, vb)
        kmem, vmem = _memory_kv(mem, wc_k[l], wc_v[l])
        x = _mix_cross(x, outs, lses, ob, kmem, vmem, out_norm_a[l], out_norm_b[l], w_o[l],
                       ln1_g[l], ln1_b[l], wc_q[l], wc_o[l], ln2_g[l], ln2_b[l], alpha)
        x = _conv_glu_ffn(x, w_up[l], conv_w[l], conv_b[l], w_down[l], ln3_g[l], ln3_b[l], alpha)
    return x


def kernel(x_prompt, x_sample, mem_prompt, mem_sample, w_in, q_norm_g, k_norm_g, out_norm_a, out_norm_b, w_o, ln1_g, ln1_b, wc_q, wc_k, wc_v, wc_o, ln2_g, ln2_b, w_up, conv_w, conv_b, w_down, ln3_g, ln3_b):
    weights = (w_in, q_norm_g, k_norm_g, out_norm_a, out_norm_b, w_o, ln1_g, ln1_b,
               wc_q, wc_k, wc_v, wc_o, ln2_g, ln2_b, w_up, conv_w, conv_b, w_down, ln3_g, ln3_b)
    return (_encoder(x_prompt, mem_prompt, *weights), _encoder(x_sample, mem_sample, *weights))
```

```python
import functools

import jax
import jax.numpy as jnp
from jax import lax
from jax.experimental import pallas as pl
from jax.experimental.pallas import tpu as pltpu

D_MODEL = 1024
HEAD_DIM = 64
N_HEADS_A = 8
N_HEADS_B = 8
N_KV_B = 2
WIDTH_A = N_HEADS_A * HEAD_DIM
WIDTH_B = N_HEADS_B * HEAD_DIM
KV_WIDTH_B = N_KV_B * HEAD_DIM
DILATED_PATTERNS = ((128, 1), (512, 4), (2048, 16))
N_HEADS_MEM = 4
HEAD_DIM_MEM = D_MODEL // N_HEADS_MEM
D_FF = 2816
GRID_W = 64
ROPE_THETA = 10000.0
LN_EPS = 1e-5
RMS_EPS = 1e-6
LOG2_E = 1.4426950408889634

LANES = 128
SUBLANES = 8
VMEM_LIMIT_BYTES = 56 * 1024 * 1024

ROW_TILE = 512
BAND_Q = 256
GQA_Q = 256
GQA_UNROLL = 8
FF_CHUNK = 256

F32 = jnp.float32
BF16 = jnp.bfloat16


def _params(*semantics):
    return pltpu.CompilerParams(dimension_semantics=semantics, vmem_limit_bytes=VMEM_LIMIT_BYTES)


def _const_spec(shape):
    zeros = (0,) * len(shape)
    return pl.BlockSpec(shape, lambda *_: zeros, pipeline_mode=pl.Buffered(1))


def _swap_within(t, half, first):
    n = t.shape[-1]
    fwd = pltpu.roll(t, n - half, axis=1)
    bwd = pltpu.roll(t, half, axis=1)
    return jnp.where(first, fwd, bwd)


def _dot(a, b):
    return jnp.dot(a, b, preferred_element_type=F32)


def _dot_nt(a, b):
    return lax.dot_general(a, b, (((1,), (1,)), ((), ())), preferred_element_type=F32)


def _proj_kernel(x_ref, w_ref, ca_ref, sa_ref, cb_ref, sb_ref, gq_ref, gk_ref, gsum_ref,
                 qa_ref, ka_ref, va_ref, qb_ref, kb_ref, vt_ref):
    rows = x_ref.shape[0]
    xb = x_ref[...].astype(BF16)
    lane = lax.broadcasted_iota(jnp.int32, (rows, LANES), 1)
    first32 = (lane % 64) < 32
    first16 = (lane % 32) < 16
    low = lane < 64
    ca, sa = ca_ref[...], sa_ref[...]
    cb, sb = cb_ref[...], sb_ref[...]
    gsum = gsum_ref[...]

    def rope_a(t, scale):
        return (t * ca + _swap_within(t, 32, first32) * sa) * scale

    def head_rms(t, g):
        sq = t * t
        hi = sq.astype(BF16)
        lo = (sq - hi.astype(F32)).astype(BF16)
        ss = _dot(hi, gsum) + _dot(lo, gsum)
        return t * lax.rsqrt(ss * (1.0 / HEAD_DIM) + RMS_EPS) * g

    def rope_b(t, scale):
        return (t * cb + _swap_within(t, 16, first16) * sb) * scale

    qscale = HEAD_DIM ** -0.5
    c0 = 0
    qa = _dot(xb, w_ref[:, c0:c0 + WIDTH_A]); c0 += WIDTH_A
    for c in range(WIDTH_A // LANES):
        sl = slice(c * LANES, (c + 1) * LANES)
        qa_ref[:, sl] = rope_a(qa[:, sl], qscale * LOG2_E)
    ka = _dot(xb, w_ref[:, c0:c0 + WIDTH_A]); c0 += WIDTH_A
    for c in range(WIDTH_A // LANES):
        sl = slice(c * LANES, (c + 1) * LANES)
        ka_ref[:, sl] = rope_a(ka[:, sl], 1.0)
    va_ref[...] = _dot(xb, w_ref[:, c0:c0 + WIDTH_A]); c0 += WIDTH_A

    qb = _dot(xb, w_ref[:, c0:c0 + WIDTH_B]); c0 += WIDTH_B
    gq = gq_ref[...]
    zero = jnp.zeros((rows, LANES), F32)
    for c in range(WIDTH_B // LANES):
        y = rope_b(head_rms(qb[:, c * LANES:(c + 1) * LANES], gq), qscale * LOG2_E)
        rolled = pltpu.roll(y, 64, axis=1)
        if (2 * c) // (N_HEADS_B // N_KV_B) == 0:
            even, odd = jnp.where(low, y, zero), jnp.where(low, rolled, zero)
        else:
            even, odd = jnp.where(low, zero, rolled), jnp.where(low, zero, y)
        qb_ref[:, (2 * c) * LANES:(2 * c + 1) * LANES] = even.astype(BF16)
        qb_ref[:, (2 * c + 1) * LANES:(2 * c + 2) * LANES] = odd.astype(BF16)

    kv = _dot(xb, w_ref[:, c0:c0 + 2 * KV_WIDTH_B])
    kb_ref[...] = rope_b(head_rms(kv[:, :KV_WIDTH_B], gk_ref[...]), 1.0).astype(BF16)
    vt = kv[:, KV_WIDTH_B:].T
    row = lax.broadcasted_iota(jnp.int32, vt.shape, 0)
    vt_ref[0] = jnp.where(row < HEAD_DIM, vt, 1.0).astype(BF16)
    vt_ref[1] = jnp.where(row < HEAD_DIM, 1.0, vt).astype(BF16)


def _rope_tables(n):
    pos = jnp.arange(n)

    def angles(p, dim):
        inv = ROPE_THETA ** (-jnp.arange(0, dim, 2, dtype=F32) / dim)
        ang = p.astype(F32)[:, None] * inv[None, :]
        return jnp.cos(ang), jnp.sin(ang)

    c, s = angles(pos, HEAD_DIM)
    ca = jnp.tile(jnp.concatenate([c, c], axis=-1), (1, 2))
    sa = jnp.tile(jnp.concatenate([-s, s], axis=-1), (1, 2))
    cr, sr = angles(pos // GRID_W, HEAD_DIM // 2)
    cc, sc = angles(pos % GRID_W, HEAD_DIM // 2)
    cb = jnp.tile(jnp.concatenate([cr, cr, cc, cc], axis=-1), (1, 2))
    sb = jnp.tile(jnp.concatenate([-sr, sr, -sc, sc], axis=-1), (1, 2))
    return ca, sa, cb, sb


def _project(x, w_in, q_norm_g, k_norm_g):
    b, n, _ = x.shape
    tm = min(ROW_TILE, n)
    ca, sa, cb, sb = _rope_tables(n)
    gq = jnp.tile(q_norm_g, 2)[None, :]
    gk = jnp.tile(k_norm_g, 2)[None, :]
    grp = jnp.arange(LANES) // HEAD_DIM
    gsum = (grp[:, None] == grp[None, :]).astype(BF16)
    in_cols = w_in.shape[1]
    tab = pl.BlockSpec((tm, LANES), lambda bi, i: (i, 0))

    def out(width):
        return pl.BlockSpec((None, tm, width), lambda bi, i: (bi, i, 0))

    return pl.pallas_call(
        _proj_kernel,
        grid=(b, n // tm),
        in_specs=[pl.BlockSpec((None, tm, D_MODEL), lambda bi, i: (bi, i, 0)),
                  _const_spec((D_MODEL, in_cols)), tab, tab, tab, tab,
                  _const_spec((1, LANES)), _const_spec((1, LANES)), _const_spec((LANES, LANES))],
        out_specs=[out(WIDTH_A), out(WIDTH_A), out(WIDTH_A), out(2 * WIDTH_B), out(KV_WIDTH_B),
                   pl.BlockSpec((None, N_KV_B, None, KV_WIDTH_B, tm), lambda bi, i: (bi, 0, i, 0, 0))],
        out_shape=[jax.ShapeDtypeStruct((b, n, WIDTH_A), F32)] * 3
        + [jax.ShapeDtypeStruct((b, n, 2 * WIDTH_B), BF16),
           jax.ShapeDtypeStruct((b, n, KV_WIDTH_B), BF16),
           jax.ShapeDtypeStruct((b, N_KV_B, n // tm, KV_WIDTH_B, tm), BF16)],
        compiler_params=_params("parallel", "parallel"),
        name="proj",
    )(x, w_in.astype(BF16), ca, sa, cb, sb, gq, gk, gsum)


def _dilated_kernel(q_ref, k_ref, v_ref, o_ref, num_ref, den_ref, max_ref, bias_ref, s_ref, top_ref):
    n = q_ref.shape[0]
    last = len(DILATED_PATTERNS) - 1
    for idx, (window, d) in enumerate(DILATED_PATTERNS):
        half = window // (2 * d)
        length = n // d
        tq = min(BAND_Q, length)
        win = min(tq + 2 * half, length)
        per_phase = length // tq
        lane = lax.broadcasted_iota(jnp.int32, (tq, LANES), 1)
        low = lane < HEAD_DIM
        vlow = lax.broadcasted_iota(jnp.int32, (win, LANES), 1) < HEAD_DIM
        qi = lax.broadcasted_iota(jnp.int32, (tq, win), 0)
        kj = lax.broadcasted_iota(jnp.int32, (tq, win), 1)
        for case, off in enumerate((0, -half, tq - win)):
            bias_ref[case, :tq, :win] = jnp.where(jnp.abs(kj + off - qi) <= half, 0.0, -jnp.inf)

        total = d * per_phase

        def place(t, d=d, half=half, length=length, tq=tq, win=win, per_phase=per_phase):
            r = t // per_phase
            q0 = (t % per_phase) * tq
            k0 = jnp.clip(q0 - half, 0, length - win)
            case = jnp.where(q0 < half, 0, jnp.where(q0 - half > length - win, 2, 1))
            return pl.ds(r + d * q0, tq, stride=d), pl.ds(r + d * k0, win, stride=d), case

        def scores(t, slot, tq=tq, win=win, low=low, place=place):
            rows, keys, case = place(t)
            q = q_ref[rows, :].astype(BF16)
            k = k_ref[keys, :].astype(BF16)
            bias = bias_ref[case, :tq, :win]
            for h, own in enumerate((low, jnp.logical_not(low))):
                s = _dot_nt(jnp.where(own, q, jnp.zeros_like(q)), k) + bias
                s_ref[slot, h, :tq, :win] = s
                top_ref[slot, h, :tq, :] = jnp.broadcast_to(jnp.max(s, axis=-1, keepdims=True), (tq, LANES))

        def finish(t, slot, tq=tq, win=win, low=low, vlow=vlow, place=place, idx=idx):
            rows, keys, _ = place(t)
            v = v_ref[keys, :]
            sums = []
            for h, vown in enumerate((vlow, jnp.logical_not(vlow))):
                vh = jnp.where(vown, v, 1.0).astype(BF16)
                top_wide = jnp.tile(top_ref[slot, h, :tq, :], (1, win // LANES))
                sums.append(_dot(jnp.exp2(s_ref[slot, h, :tq, :win] - top_wide).astype(BF16), vh))
            num = jnp.where(low, sums[0], sums[1])
            den = jnp.where(low, pltpu.roll(sums[0], HEAD_DIM, axis=1),
                            pltpu.roll(sums[1], HEAD_DIM, axis=1))
            top = jnp.where(low, top_ref[slot, 0, :tq, :], top_ref[slot, 1, :tq, :])
            if idx > 0:
                old_top = max_ref[rows, :]
                new_top = jnp.maximum(old_top, top)
                a_old, a_new = jnp.exp2(old_top - new_top), jnp.exp2(top - new_top)
                num = num_ref[rows, :] * a_old + num * a_new
                den = den_ref[rows, :] * a_old + den * a_new
                top = new_top
            if idx < last:
                num_ref[rows, :] = num
                den_ref[rows, :] = den
                max_ref[rows, :] = top
            else:
                num_ref[rows, :] = num / den

        def pair(t2, carry, scores=scores, finish=finish, total=total):
            t = 2 * t2
            scores(t + 1, 1)
            finish(t, 0)
            scores(jnp.minimum(t + 2, total - 1), 0)
            finish(t + 1, 1)
            return carry

        scores(0, 0)
        lax.fori_loop(0, total // 2, pair, 0)
    o_ref[...] = num_ref[...].astype(o_ref.dtype)


def _dilated_attention(qa, ka, va):
    b, n, width = qa.shape
    spec = pl.BlockSpec((None, n, LANES), lambda bi, h: (bi, 0, h))
    tq = max(min(BAND_Q, n // d) for _, d in DILATED_PATTERNS)
    win = max(min(min(BAND_Q, n // d) + w // d, n // d) for w, d in DILATED_PATTERNS)
    assert all((n // d // min(BAND_Q, n // d)) * d % 2 == 0 for _, d in DILATED_PATTERNS)
    return pl.pallas_call(
        _dilated_kernel,
        grid=(b, width // LANES),
        in_specs=[spec, spec, spec],
        out_specs=spec,
        out_shape=jax.ShapeDtypeStruct((b, n, width), BF16),
        scratch_shapes=[pltpu.VMEM((n, LANES), F32)] * 3 + [pltpu.VMEM((3, tq, win), F32),
                        pltpu.VMEM((2, 2, tq, win), F32), pltpu.VMEM((2, 2, tq, LANES), F32)],
        compiler_params=_params("parallel", "parallel"),
        name="dilated",
    )(qa, ka, va)


def _gqa_kernel(q_ref, k_ref, vt_ref, o_ref, s_ref):
    tq = q_ref.shape[0]
    chunks, _, tk = vt_ref.shape
    qs = (q_ref[:, :LANES], q_ref[:, LANES:])

    def scores(c, slot):
        k = k_ref[pl.ds(pl.multiple_of(c * tk, tk), tk), :]
        tops = []
        for h, q in enumerate(qs):
            s = _dot_nt(k, q)
            s_ref[slot, h] = s
            tops.append(jnp.max(s, axis=0, keepdims=True))
        return tuple(tops)

    def accumulate(c, slot, tops, state):
        vt = vt_ref[c]
        new = []
        for h, (m, acc) in enumerate(state):
            m_new = jnp.maximum(m, tops[h])
            p = jnp.exp2(s_ref[slot, h] - m_new).astype(BF16)
            new.append((m_new, jnp.exp2(m - m_new) * acc + _dot(vt, p)))
        return tuple(new)

    unroll = min(GQA_UNROLL, chunks)

    def trip(t, carry):
        tops, state = carry
        for u in range(unroll):
            c = t * unroll + u
            nxt = scores(jnp.minimum(c + 1, chunks - 1), (u + 1) % 2)
            state = accumulate(c, u % 2, tops, state)
            tops = nxt
        return tops, state

    state = ((jnp.full((1, tq), -jnp.inf, F32), jnp.zeros((LANES, tq), F32)),) * 2
    _, carry = lax.fori_loop(0, chunks // unroll, trip, (scores(0, 0), state))
    kv0 = pl.program_id(2) < (N_HEADS_B // N_KV_B) // 2
    outs = []
    for _, acc in carry:
        num = jnp.where(kv0, acc[:HEAD_DIM], acc[HEAD_DIM:])
        den = jnp.where(kv0, acc[HEAD_DIM:HEAD_DIM + 1], acc[0:1])
        outs.append(num / den)
    o_ref[...] = jnp.concatenate(outs, axis=0).T.astype(o_ref.dtype)


def _gqa_attention(qb, kb, vt):
    b, n, _ = qb.shape
    _, _, chunks, _, tk = vt.shape
    assert chunks % 2 == 0 and chunks % min(GQA_UNROLL, chunks) == 0, "slots alternate by chunk parity"
    tq = min(GQA_Q, n)
    pairs_per_kv = (N_HEADS_B // N_KV_B) // 2
    return pl.pallas_call(
        _gqa_kernel,
        grid=(b, n // tq, WIDTH_B // LANES),
        in_specs=[pl.BlockSpec((None, tq, 2 * LANES), lambda bi, i, j: (bi, i, j)),
                  pl.BlockSpec((None, n, KV_WIDTH_B), lambda bi, i, j: (bi, 0, 0)),
                  pl.BlockSpec((None, None, chunks, KV_WIDTH_B, tk),
                               lambda bi, i, j: (bi, j // pairs_per_kv, 0, 0, 0))],
        out_specs=pl.BlockSpec((None, tq, LANES), lambda bi, i, j: (bi, i, j)),
        out_shape=jax.ShapeDtypeStruct((b, n, WIDTH_B), BF16),
        scratch_shapes=[pltpu.VMEM((2, 2, tk, tq), F32)],
        compiler_params=_params("parallel", "parallel", "parallel"),
        name="gqa",
    )(qb, kb, vt)


def _memkv_kernel(mem_ref, wk_ref, wv_ref, k_ref, v_ref):
    mb = mem_ref[...].astype(BF16)
    k_ref[...] = _dot(mb, wk_ref[...]).astype(BF16)
    v_ref[...] = _dot(mb, wv_ref[...]).astype(BF16)


def _memory_kv(mem, wk, wv):
    b, m, _ = mem.shape
    spec = pl.BlockSpec((None, m, D_MODEL), lambda bi: (bi, 0, 0))
    wspec = _const_spec((D_MODEL, D_MODEL))
    return pl.pallas_call(
        _memkv_kernel,
        grid=(b,),
        in_specs=[spec, wspec, wspec],
        out_specs=[spec, spec],
        out_shape=[jax.ShapeDtypeStruct((b, m, D_MODEL), BF16)] * 2,
        compiler_params=_params("parallel"),
        name="memkv",
    )(mem, wk.astype(BF16), wv.astype(BF16))


def _layer_norm(x, g, b):
    mu = jnp.mean(x, axis=-1, keepdims=True)
    xc = x - mu
    var = jnp.mean(xc * xc, axis=-1, keepdims=True)
    return xc * lax.rsqrt(var + LN_EPS) * g + b


def _rms_norm(x, g):
    ms = jnp.mean(x * x, axis=-1, keepdims=True)
    return x * lax.rsqrt(ms + RMS_EPS) * g


def _mix_kernel(x_ref, oa_ref, ob_ref,
                ga_ref, gb_ref, wo_ref, g1_ref, b1_ref, wq_ref, km_ref, vm_ref, wco_ref,
                g2_ref, b2_ref, y_ref, *, alpha):
    oa = _rms_norm(oa_ref[...].astype(F32), ga_ref[...]).astype(BF16)
    ob = _rms_norm(ob_ref[...].astype(F32), gb_ref[...]).astype(BF16)
    mix = _dot(oa, wo_ref[:WIDTH_A, :]) + _dot(ob, wo_ref[WIDTH_A:, :])
    x1 = _layer_norm(alpha * x_ref[...] + mix, g1_ref[...], b1_ref[...])

    q = (_dot(x1.astype(BF16), wq_ref[...]) * (HEAD_DIM_MEM ** -0.5)).astype(BF16)
    heads = []
    for h in range(N_HEADS_MEM):
        sl = slice(h * HEAD_DIM_MEM, (h + 1) * HEAD_DIM_MEM)
        s = _dot_nt(q[:, sl], km_ref[:, sl])
        e = jnp.exp(s - jnp.max(s, axis=-1, keepdims=True))
        den = jnp.sum(e, axis=-1, keepdims=True)
        heads.append((_dot(e.astype(BF16), vm_ref[:, sl]) / den).astype(BF16))
    cross = _dot(jnp.concatenate(heads, axis=-1), wco_ref[...])
    y_ref[...] = _layer_norm(alpha * x1 + cross, g2_ref[...], b2_ref[...])


def _mix_cross(x, oa, ob, kmem, vmem, out_norm_a, out_norm_b, w_o, ln1_g, ln1_b,
               wc_q, wc_o, ln2_g, ln2_b, alpha):
    b, n, _ = x.shape
    tm = min(ROW_TILE, n)
    m = kmem.shape[1]

    def rows(width):
        return pl.BlockSpec((None, tm, width), lambda bi, i: (bi, i, 0))

    vec = lambda width: _const_spec((1, width))
    memspec = pl.BlockSpec((None, m, D_MODEL), lambda bi, i: (bi, 0, 0))
    wspec = _const_spec((D_MODEL, D_MODEL))
    return pl.pallas_call(
        functools.partial(_mix_kernel, alpha=alpha),
        grid=(b, n // tm),
        in_specs=[rows(D_MODEL), rows(WIDTH_A), rows(WIDTH_B), vec(WIDTH_A), vec(WIDTH_B),
                  wspec, vec(D_MODEL), vec(D_MODEL), wspec, memspec, memspec, wspec,
                  vec(D_MODEL), vec(D_MODEL)],
        out_specs=rows(D_MODEL),
        out_shape=jax.ShapeDtypeStruct((b, n, D_MODEL), F32),
        compiler_params=_params("parallel", "parallel"),
        name="mix",
    )(x, oa, ob, out_norm_a[None, :], out_norm_b[None, :], w_o.astype(BF16),
      ln1_g[None, :], ln1_b[None, :], wc_q.astype(BF16), kmem, vmem, wc_o.astype(BF16),
      ln2_g[None, :], ln2_b[None, :])


def _ffn_kernel(x_ref, prev_ref, next_ref, wup_ref, cw_ref, cb_ref, wdn_ref, g_ref, b_ref,
                y_ref, h_ref, *, alpha):
    rows = x_ref.shape[0]
    i = pl.program_id(1)
    x = x_ref[...]
    prev = jnp.where(i > 0, prev_ref[...], 0.0)
    nxt = jnp.where(i < pl.num_programs(1) - 1, next_ref[...], 0.0)
    xb = x.astype(BF16)
    xe = jnp.concatenate([prev, x, nxt], axis=0).astype(BF16)
    ext = rows + 2 * SUBLANES
    for c in range(D_FF // FF_CHUNK):
        sl = slice(c * FF_CHUNK, (c + 1) * FF_CHUNK)
        gate = _dot(xe, wup_ref[:, sl])
        before = pltpu.roll(gate, 1, axis=0)[SUBLANES:SUBLANES + rows]
        after = pltpu.roll(gate, ext - 1, axis=0)[SUBLANES:SUBLANES + rows]
        here = gate[SUBLANES:SUBLANES + rows]
        conv = (before * cw_ref[0:1, sl] + here * cw_ref[1:2, sl] + after * cw_ref[2:3, sl]
                + cb_ref[:, sl])
        act = 0.5 * conv * (1.0 + lax.erf(conv * (2.0 ** -0.5)))
        val = _dot(xb, wup_ref[:, D_FF + c * FF_CHUNK:D_FF + (c + 1) * FF_CHUNK])
        h_ref[:, sl] = (act * val).astype(BF16)
    y = _dot(h_ref[...], wdn_ref[...])
    y_ref[...] = _layer_norm(alpha * x + y, g_ref[...], b_ref[...])


def _conv_glu_ffn(x, w_up, conv_w, conv_b, w_down, ln_g, ln_b, alpha):
    b, n, _ = x.shape
    tm = min(ROW_TILE, n)
    per = tm // SUBLANES
    last = n // SUBLANES - 1
    vec = _const_spec((1, D_MODEL))
    return pl.pallas_call(
        functools.partial(_ffn_kernel, alpha=alpha),
        grid=(b, n // tm),
        in_specs=[pl.BlockSpec((None, tm, D_MODEL), lambda bi, i: (bi, i, 0)),
                  pl.BlockSpec((None, SUBLANES, D_MODEL),
                               lambda bi, i: (bi, jnp.maximum(i * per - 1, 0), 0)),
                  pl.BlockSpec((None, SUBLANES, D_MODEL),
                               lambda bi, i: (bi, jnp.minimum((i + 1) * per, last), 0)),
                  _const_spec((D_MODEL, 2 * D_FF)), _const_spec((3, D_FF)), _const_spec((1, D_FF)),
                  _const_spec((D_FF, D_MODEL)), vec, vec],
        out_specs=pl.BlockSpec((None, tm, D_MODEL), lambda bi, i: (bi, i, 0)),
        out_shape=jax.ShapeDtypeStruct((b, n, D_MODEL), F32),
        scratch_shapes=[pltpu.VMEM((tm, D_FF), BF16)],
        compiler_params=_params("parallel", "parallel"),
        name="ffn",
    )(x, x, x, w_up.astype(BF16), conv_w, conv_b[None, :], w_down.astype(BF16),
      ln_g[None, :], ln_b[None, :])


def _encoder(x, mem, w_in, q_norm_g, k_norm_g, out_norm_a, out_norm_b, w_o, ln1_g, ln1_b,
             wc_q, wc_k, wc_v, wc_o, ln2_g, ln2_b, w_up, conv_w, conv_b, w_down, ln3_g, ln3_b):
    depth = w_in.shape[0]
    alpha = (2 * depth) ** 0.25
    for l in range(depth):
        qa, ka, va, qb, kb, vb = _project(x, w_in[l], q_norm_g[l], k_norm_g[l])
        oa = _dilated_attention(qa, ka, va)
        ob = _gqa_attention(qb, kb, vb)
        kmem, vmem = _memory_kv(mem, wc_k[l], wc_v[l])
        x = _mix_cross(x, oa, ob, kmem, vmem, out_norm_a[l], out_norm_b[l], w_o[l],
                       ln1_g[l], ln1_b[l], wc_q[l], wc_o[l], ln2_g[l], ln2_b[l], alpha)
        x = _conv_glu_ffn(x, w_up[l], conv_w[l], conv_b[l], w_down[l], ln3_g[l], ln3_b[l], alpha)
    return x


def kernel(x_prompt, x_sample, mem_prompt, mem_sample, w_in, q_norm_g, k_norm_g, out_norm_a, out_norm_b, w_o, ln1_g, ln1_b, wc_q, wc_k, wc_v, wc_o, ln2_g, ln2_b, w_up, conv_w, conv_b, w_down, ln3_g, ln3_b):
    weights = (w_in, q_norm_g, k_norm_g, out_norm_a, out_norm_b, w_o, ln1_g, ln1_b,
               wc_q, wc_k, wc_v, wc_o, ln2_g, ln2_b, w_up, conv_w, conv_b, w_down, ln3_g, ln3_b)
    return (_encoder(x_prompt, mem_prompt, *weights), _encoder(x_sample, mem_sample, *weights))
```

```python
import functools

import jax
import jax.numpy as jnp
from jax import lax
from jax.experimental import pallas as pl
from jax.experimental.pallas import tpu as pltpu

D_MODEL = 1024
HEAD_DIM = 64
N_HEADS_A = 8
N_HEADS_B = 8
N_KV_B = 2
WIDTH_A = N_HEADS_A * HEAD_DIM
WIDTH_B = N_HEADS_B * HEAD_DIM
KV_WIDTH_B = N_KV_B * HEAD_DIM
DILATED_PATTERNS = ((128, 1), (512, 4), (2048, 16))
N_HEADS_MEM = 4
HEAD_DIM_MEM = D_MODEL // N_HEADS_MEM
D_FF = 2816
GRID_W = 64
ROPE_THETA = 10000.0
LN_EPS = 1e-5
RMS_EPS = 1e-6
LOG2_E = 1.4426950408889634

LANES = 128
SUBLANES = 8
VMEM_LIMIT_BYTES = 56 * 1024 * 1024

ROW_TILE = 512
BAND_Q = 256
GQA_Q = 512
GQA_UNROLL = 8
FF_CHUNK = 256

F32 = jnp.float32
BF16 = jnp.bfloat16


def _params(*semantics):
    return pltpu.CompilerParams(dimension_semantics=semantics, vmem_limit_bytes=VMEM_LIMIT_BYTES)


def _const_spec(shape):
    zeros = (0,) * len(shape)
    return pl.BlockSpec(shape, lambda *_: zeros, pipeline_mode=pl.Buffered(1))


def _swap_within(t, half, first):
    n = t.shape[-1]
    fwd = pltpu.roll(t, n - half, axis=1)
    bwd = pltpu.roll(t, half, axis=1)
    return jnp.where(first, fwd, bwd)


def _dot(a, b):
    return jnp.dot(a, b, preferred_element_type=F32)


def _dot_nt(a, b):
    return lax.dot_general(a, b, (((1,), (1,)), ((), ())), preferred_element_type=F32)


def _proj_kernel(x_ref, w_ref, ca_ref, sa_ref, cb_ref, sb_ref, gq_ref, gk_ref, gsum_ref,
                 qa_ref, ka_ref, va_ref, qb_ref, kb_ref, vt_ref):
    rows = x_ref.shape[0]
    xb = x_ref[...].astype(BF16)
    lane = lax.broadcasted_iota(jnp.int32, (rows, LANES), 1)
    first32 = (lane % 64) < 32
    first16 = (lane % 32) < 16
    low = lane < 64
    ca, sa = ca_ref[...], sa_ref[...]
    cb, sb = cb_ref[...], sb_ref[...]
    gsum = gsum_ref[...]

    def rope_a(t, scale):
        return (t * ca + _swap_within(t, 32, first32) * sa) * scale

    def head_rms(t, g):
        sq = t * t
        hi = sq.astype(BF16)
        lo = (sq - hi.astype(F32)).astype(BF16)
        ss = _dot(hi, gsum) + _dot(lo, gsum)
        return t * lax.rsqrt(ss * (1.0 / HEAD_DIM) + RMS_EPS) * g

    def rope_b(t, scale):
        return (t * cb + _swap_within(t, 16, first16) * sb) * scale

    qscale = HEAD_DIM ** -0.5
    c0 = 0
    qa = _dot(xb, w_ref[:, c0:c0 + WIDTH_A]); c0 += WIDTH_A
    for c in range(WIDTH_A // LANES):
        sl = slice(c * LANES, (c + 1) * LANES)
        qa_ref[:, sl] = rope_a(qa[:, sl], qscale * LOG2_E)
    ka = _dot(xb, w_ref[:, c0:c0 + WIDTH_A]); c0 += WIDTH_A
    for c in range(WIDTH_A // LANES):
        sl = slice(c * LANES, (c + 1) * LANES)
        ka_ref[:, sl] = rope_a(ka[:, sl], 1.0)
    va_ref[...] = _dot(xb, w_ref[:, c0:c0 + WIDTH_A]); c0 += WIDTH_A

    qb = _dot(xb, w_ref[:, c0:c0 + WIDTH_B]); c0 += WIDTH_B
    gq = gq_ref[...]
    zero = jnp.zeros((rows, LANES), F32)
    for c in range(WIDTH_B // LANES):
        y = rope_b(head_rms(qb[:, c * LANES:(c + 1) * LANES], gq), qscale * LOG2_E)
        rolled = pltpu.roll(y, 64, axis=1)
        if (2 * c) // (N_HEADS_B // N_KV_B) == 0:
            even, odd = jnp.where(low, y, zero), jnp.where(low, rolled, zero)
        else:
            even, odd = jnp.where(low, zero, rolled), jnp.where(low, zero, y)
        qb_ref[:, (2 * c) * LANES:(2 * c + 1) * LANES] = even.astype(BF16)
        qb_ref[:, (2 * c + 1) * LANES:(2 * c + 2) * LANES] = odd.astype(BF16)

    kv = _dot(xb, w_ref[:, c0:c0 + 2 * KV_WIDTH_B])
    kb_ref[...] = rope_b(head_rms(kv[:, :KV_WIDTH_B], gk_ref[...]), 1.0).astype(BF16)
    vt = kv[:, KV_WIDTH_B:].T
    row = lax.broadcasted_iota(jnp.int32, vt.shape, 0)
    vt_ref[0] = jnp.where(row < HEAD_DIM, vt, 1.0).astype(BF16)
    vt_ref[1] = jnp.where(row < HEAD_DIM, 1.0, vt).astype(BF16)


def _rope_tables(n):
    pos = jnp.arange(n)

    def angles(p, dim):
        inv = ROPE_THETA ** (-jnp.arange(0, dim, 2, dtype=F32) / dim)
        ang = p.astype(F32)[:, None] * inv[None, :]
        return jnp.cos(ang), jnp.sin(ang)

    c, s = angles(pos, HEAD_DIM)
    ca = jnp.tile(jnp.concatenate([c, c], axis=-1), (1, 2))
    sa = jnp.tile(jnp.concatenate([-s, s], axis=-1), (1, 2))
    cr, sr = angles(pos // GRID_W, HEAD_DIM // 2)
    cc, sc = angles(pos % GRID_W, HEAD_DIM // 2)
    cb = jnp.tile(jnp.concatenate([cr, cr, cc, cc], axis=-1), (1, 2))
    sb = jnp.tile(jnp.concatenate([-sr, sr, -sc, sc], axis=-1), (1, 2))
    return ca, sa, cb, sb


def _project(x, w_in, q_norm_g, k_norm_g):
    b, n, _ = x.shape
    tm = min(ROW_TILE, n)
    ca, sa, cb, sb = _rope_tables(n)
    gq = jnp.tile(q_norm_g, 2)[None, :]
    gk = jnp.tile(k_norm_g, 2)[None, :]
    grp = jnp.arange(LANES) // HEAD_DIM
    gsum = (grp[:, None] == grp[None, :]).astype(BF16)
    in_cols = w_in.shape[1]
    tab = pl.BlockSpec((tm, LANES), lambda bi, i: (i, 0))

    def out(width):
        return pl.BlockSpec((None, tm, width), lambda bi, i: (bi, i, 0))

    return pl.pallas_call(
        _proj_kernel,
        grid=(b, n // tm),
        in_specs=[pl.BlockSpec((None, tm, D_MODEL), lambda bi, i: (bi, i, 0)),
                  _const_spec((D_MODEL, in_cols)), tab, tab, tab, tab,
                  _const_spec((1, LANES)), _const_spec((1, LANES)), _const_spec((LANES, LANES))],
        out_specs=[out(WIDTH_A), out(WIDTH_A), out(WIDTH_A), out(2 * WIDTH_B), out(KV_WIDTH_B),
                   pl.BlockSpec((None, N_KV_B, None, KV_WIDTH_B, tm), lambda bi, i: (bi, 0, i, 0, 0))],
        out_shape=[jax.ShapeDtypeStruct((b, n, WIDTH_A), F32)] * 3
        + [jax.ShapeDtypeStruct((b, n, 2 * WIDTH_B), BF16),
           jax.ShapeDtypeStruct((b, n, KV_WIDTH_B), BF16),
           jax.ShapeDtypeStruct((b, N_KV_B, n // tm, KV_WIDTH_B, tm), BF16)],
        compiler_params=_params("parallel", "parallel"),
        name="proj",
    )(x, w_in.astype(BF16), ca, sa, cb, sb, gq, gk, gsum)


def _dilated_kernel(q_ref, k_ref, v_ref, o_ref, num_ref, den_ref, max_ref, bias_ref, s_ref, top_ref):
    n = q_ref.shape[0]
    last = len(DILATED_PATTERNS) - 1
    for idx, (window, d) in enumerate(DILATED_PATTERNS):
        half = window // (2 * d)
        length = n // d
        tq = min(BAND_Q, length)
        win = min(tq + 2 * half, length)
        per_phase = length // tq
        lane = lax.broadcasted_iota(jnp.int32, (tq, LANES), 1)
        low = lane < HEAD_DIM
        vlow = lax.broadcasted_iota(jnp.int32, (win, LANES), 1) < HEAD_DIM
        qi = lax.broadcasted_iota(jnp.int32, (tq, win), 0)
        kj = lax.broadcasted_iota(jnp.int32, (tq, win), 1)
        for case, off in enumerate((0, -half, tq - win)):
            bias_ref[case, :tq, :win] = jnp.where(jnp.abs(kj + off - qi) <= half, 0.0, -jnp.inf)

        total = d * per_phase

        def place(t, d=d, half=half, length=length, tq=tq, win=win, per_phase=per_phase):
            r = t // per_phase
            q0 = (t % per_phase) * tq
            k0 = jnp.clip(q0 - half, 0, length - win)
            case = jnp.where(q0 < half, 0, jnp.where(q0 - half > length - win, 2, 1))
            return pl.ds(r + d * q0, tq, stride=d), pl.ds(r + d * k0, win, stride=d), case

        def scores(t, slot, tq=tq, win=win, low=low, place=place):
            rows, keys, case = place(t)
            q = q_ref[rows, :].astype(BF16)
            k = k_ref[keys, :].astype(BF16)
            bias = bias_ref[case, :tq, :win]
            for h, own in enumerate((low, jnp.logical_not(low))):
                s = _dot_nt(jnp.where(own, q, jnp.zeros_like(q)), k) + bias
                s_ref[slot, h, :tq, :win] = s
                top_ref[slot, h, :tq, :] = jnp.broadcast_to(jnp.max(s, axis=-1, keepdims=True), (tq, LANES))

        def finish(t, slot, tq=tq, win=win, low=low, vlow=vlow, place=place, idx=idx):
            rows, keys, _ = place(t)
            v = v_ref[keys, :]
            sums = []
            for h, vown in enumerate((vlow, jnp.logical_not(vlow))):
                vh = jnp.where(vown, v, 1.0).astype(BF16)
                top_wide = jnp.tile(top_ref[slot, h, :tq, :], (1, win // LANES))
                sums.append(_dot(jnp.exp2(s_ref[slot, h, :tq, :win] - top_wide).astype(BF16), vh))
            num = jnp.where(low, sums[0], sums[1])
            den = jnp.where(low, pltpu.roll(sums[0], HEAD_DIM, axis=1),
                            pltpu.roll(sums[1], HEAD_DIM, axis=1))
            top = jnp.where(low, top_ref[slot, 0, :tq, :], top_ref[slot, 1, :tq, :])
            if idx > 0:
                old_top = max_ref[rows, :]
                new_top = jnp.maximum(old_top, top)
                a_old, a_new = jnp.exp2(old_top - new_top), jnp.exp2(top - new_top)
                num = num_ref[rows, :] * a_old + num * a_new
                den = den_ref[rows, :] * a_old + den * a_new
                top = new_top
            if idx < last:
                num_ref[rows, :] = num
                den_ref[rows, :] = den
                max_ref[rows, :] = top
            else:
                num_ref[rows, :] = num / den

        def pair(t2, carry, scores=scores, finish=finish, total=total):
            t = 2 * t2
            scores(t + 1, 1)
            finish(t, 0)
            scores(jnp.minimum(t + 2, total - 1), 0)
            finish(t + 1, 1)
            return carry

        scores(0, 0)
        lax.fori_loop(0, total // 2, pair, 0)
    o_ref[...] = num_ref[...].astype(o_ref.dtype)


def _dilated_attention(qa, ka, va):
    b, n, width = qa.shape
    spec = pl.BlockSpec((None, n, LANES), lambda bi, h: (bi, 0, h))
    tq = max(min(BAND_Q, n // d) for _, d in DILATED_PATTERNS)
    win = max(min(min(BAND_Q, n // d) + w // d, n // d) for w, d in DILATED_PATTERNS)
    assert all((n // d // min(BAND_Q, n // d)) * d % 2 == 0 for _, d in DILATED_PATTERNS)
    return pl.pallas_call(
        _dilated_kernel,
        grid=(b, width // LANES),
        in_specs=[spec, spec, spec],
        out_specs=spec,
        out_shape=jax.ShapeDtypeStruct((b, n, width), BF16),
        scratch_shapes=[pltpu.VMEM((n, LANES), F32)] * 3 + [pltpu.VMEM((3, tq, win), F32),
                        pltpu.VMEM((2, 2, tq, win), F32), pltpu.VMEM((2, 2, tq, LANES), F32)],
        compiler_params=_params("parallel", "parallel"),
        name="dilated",
    )(qa, ka, va)


def _gqa_kernel(q_ref, k_ref, vt_ref, o_ref, s_ref):
    tq = q_ref.shape[0]
    chunks, _, tk = vt_ref.shape
    qs = (q_ref[:, :LANES], q_ref[:, LANES:])

    def scores(c, slot):
        k = k_ref[pl.ds(pl.multiple_of(c * tk, tk), tk), :]
        tops = []
        for h, q in enumerate(qs):
            s = _dot_nt(k, q)
            s_ref[slot, h] = s
            tops.append(jnp.max(s, axis=0, keepdims=True))
        return tuple(tops)

    def accumulate(c, slot, tops, state):
        vt = vt_ref[c]
        new = []
        for h, (m, acc) in enumerate(state):
            m_new = jnp.maximum(m, tops[h])
            p = jnp.exp2(s_ref[slot, h] - m_new).astype(BF16)
            new.append((m_new, jnp.exp2(m - m_new) * acc + _dot(vt, p)))
        return tuple(new)

    unroll = min(GQA_UNROLL, chunks)

    def trip(t, carry):
        tops, state = carry
        for u in range(unroll):
            c = t * unroll + u
            nxt = scores(jnp.minimum(c + 1, chunks - 1), (u + 1) % 2)
            state = accumulate(c, u % 2, tops, state)
            tops = nxt
        return tops, state

    state = ((jnp.full((1, tq), -jnp.inf, F32), jnp.zeros((LANES, tq), F32)),) * 2
    _, carry = lax.fori_loop(0, chunks // unroll, trip, (scores(0, 0), state))
    kv0 = pl.program_id(2) < (N_HEADS_B // N_KV_B) // 2
    outs = []
    for _, acc in carry:
        num = jnp.where(kv0, acc[:HEAD_DIM], acc[HEAD_DIM:])
        den = jnp.where(kv0, acc[HEAD_DIM:HEAD_DIM + 1], acc[0:1])
        outs.append(num / den)
    o_ref[...] = jnp.concatenate(outs, axis=0).T.astype(o_ref.dtype)


def _gqa_attention(qb, kb, vt):
    b, n, _ = qb.shape
    _, _, chunks, _, tk = vt.shape
    assert chunks % 2 == 0 and chunks % min(GQA_UNROLL, chunks) == 0, "slots alternate by chunk parity"
    tq = min(GQA_Q, n)
    pairs_per_kv = (N_HEADS_B // N_KV_B) // 2
    return pl.pallas_call(
        _gqa_kernel,
        grid=(b, n // tq, WIDTH_B // LANES),
        in_specs=[pl.BlockSpec((None, tq, 2 * LANES), lambda bi, i, j: (bi, i, j)),
                  pl.BlockSpec((None, n, KV_WIDTH_B), lambda bi, i, j: (bi, 0, 0)),
                  pl.BlockSpec((None, None, chunks, KV_WIDTH_B, tk),
                               lambda bi, i, j: (bi, j // pairs_per_kv, 0, 0, 0))],
        out_specs=pl.BlockSpec((None, tq, LANES), lambda bi, i, j: (bi, i, j)),
        out_shape=jax.ShapeDtypeStruct((b, n, WIDTH_B), BF16),
        scratch_shapes=[pltpu.VMEM((2, 2, tk, tq), F32)],
        compiler_params=_params("parallel", "parallel", "parallel"),
        name="gqa",
    )(qb, kb, vt)


def _memkv_kernel(mem_ref, wk_ref, wv_ref, k_ref, v_ref):
    mb = mem_ref[...].astype(BF16)
    k_ref[...] = _dot(mb, wk_ref[...]).astype(BF16)
    v_ref[...] = _dot(mb, wv_ref[...]).astype(BF16)


def _memory_kv(mem, wk, wv):
    b, m, _ = mem.shape
    spec = pl.BlockSpec((None, m, D_MODEL), lambda bi: (bi, 0, 0))
    wspec = _const_spec((D_MODEL, D_MODEL))
    return pl.pallas_call(
        _memkv_kernel,
        grid=(b,),
        in_specs=[spec, wspec, wspec],
        out_specs=[spec, spec],
        out_shape=[jax.ShapeDtypeStruct((b, m, D_MODEL), BF16)] * 2,
        compiler_params=_params("parallel"),
        name="memkv",
    )(mem, wk.astype(BF16), wv.astype(BF16))


def _layer_norm(x, g, b):
    mu = jnp.mean(x, axis=-1, keepdims=True)
    xc = x - mu
    var = jnp.mean(xc * xc, axis=-1, keepdims=True)
    return xc * lax.rsqrt(var + LN_EPS) * g + b


def _rms_norm(x, g):
    ms = jnp.mean(x * x, axis=-1, keepdims=True)
    return x * lax.rsqrt(ms + RMS_EPS) * g


def _mix_kernel(x_ref, oa_ref, ob_ref,
                ga_ref, gb_ref, wo_ref, g1_ref, b1_ref, wq_ref, km_ref, vm_ref, wco_ref,
                g2_ref, b2_ref, y_ref, *, alpha):
    oa = _rms_norm(oa_ref[...].astype(F32), ga_ref[...]).astype(BF16)
    ob = _rms_norm(ob_ref[...].astype(F32), gb_ref[...]).astype(BF16)
    mix = _dot(oa, wo_ref[:WIDTH_A, :]) + _dot(ob, wo_ref[WIDTH_A:, :])
    x1 = _layer_norm(alpha * x_ref[...] + mix, g1_ref[...], b1_ref[...])

    q = (_dot(x1.astype(BF16), wq_ref[...]) * (HEAD_DIM_MEM ** -0.5)).astype(BF16)
    heads = []
    for h in range(N_HEADS_MEM):
        sl = slice(h * HEAD_DIM_MEM, (h + 1) * HEAD_DIM_MEM)
        s = _dot_nt(q[:, sl], km_ref[:, sl])
        e = jnp.exp(s - jnp.max(s, axis=-1, keepdims=True))
        den = jnp.sum(e, axis=-1, keepdims=True)
        heads.append((_dot(e.astype(BF16), vm_ref[:, sl]) / den).astype(BF16))
    cross = _dot(jnp.concatenate(heads, axis=-1), wco_ref[...])
    y_ref[...] = _layer_norm(alpha * x1 + cross, g2_ref[...], b2_ref[...])


def _mix_cross(x, oa, ob, kmem, vmem, out_norm_a, out_norm_b, w_o, ln1_g, ln1_b,
               wc_q, wc_o, ln2_g, ln2_b, alpha):
    b, n, _ = x.shape
    tm = min(ROW_TILE, n)
    m = kmem.shape[1]

    def rows(width):
        return pl.BlockSpec((None, tm, width), lambda bi, i: (bi, i, 0))

    vec = lambda width: _const_spec((1, width))
    memspec = pl.BlockSpec((None, m, D_MODEL), lambda bi, i: (bi, 0, 0))
    wspec = _const_spec((D_MODEL, D_MODEL))
    return pl.pallas_call(
        functools.partial(_mix_kernel, alpha=alpha),
        grid=(b, n // tm),
        in_specs=[rows(D_MODEL), rows(WIDTH_A), rows(WIDTH_B), vec(WIDTH_A), vec(WIDTH_B),
                  wspec, vec(D_MODEL), vec(D_MODEL), wspec, memspec, memspec, wspec,
                  vec(D_MODEL), vec(D_MODEL)],
        out_specs=rows(D_MODEL),
        out_shape=jax.ShapeDtypeStruct((b, n, D_MODEL), F32),
        compiler_params=_params("parallel", "parallel"),
        name="mix",
    )(x, oa, ob, out_norm_a[None, :], out_norm_b[None, :], w_o.astype(BF16),
      ln1_g[None, :], ln1_b[None, :], wc_q.astype(BF16), kmem, vmem, wc_o.astype(BF16),
      ln2_g[None, :], ln2_b[None, :])


def _ffn_kernel(x_ref, prev_ref, next_ref, wup_ref, cw_ref, cb_ref, wdn_ref, g_ref, b_ref,
                y_ref, h_ref, *, alpha):
    rows = x_ref.shape[0]
    i = pl.program_id(1)
    x = x_ref[...]
    prev = jnp.where(i > 0, prev_ref[...], 0.0)
    nxt = jnp.where(i < pl.num_programs(1) - 1, next_ref[...], 0.0)
    xb = x.astype(BF16)
    xe = jnp.concatenate([prev, x, nxt], axis=0).astype(BF16)
    ext = rows + 2 * SUBLANES
    for c in range(D_FF // FF_CHUNK):
        sl = slice(c * FF_CHUNK, (c + 1) * FF_CHUNK)
        gate = _dot(xe, wup_ref[:, sl])
        before = pltpu.roll(gate, 1, axis=0)[SUBLANES:SUBLANES + rows]
        after = pltpu.roll(gate, ext - 1, axis=0)[SUBLANES:SUBLANES + rows]
        here = gate[SUBLANES:SUBLANES + rows]
        conv = (before * cw_ref[0:1, sl] + here * cw_ref[1:2, sl] + after * cw_ref[2:3, sl]
                + cb_ref[:, sl])
        act = 0.5 * conv * (1.0 + lax.erf(conv * (2.0 ** -0.5)))
        val = _dot(xb, wup_ref[:, D_FF + c * FF_CHUNK:D_FF + (c + 1) * FF_CHUNK])
        h_ref[:, sl] = (act * val).astype(BF16)
    y = _dot(h_ref[...], wdn_ref[...])
    y_ref[...] = _layer_norm(alpha * x + y, g_ref[...], b_ref[...])


def _conv_glu_ffn(x, w_up, conv_w, conv_b, w_down, ln_g, ln_b, alpha):
    b, n, _ = x.shape
    tm = min(ROW_TILE, n)
    per = tm // SUBLANES
    last = n // SUBLANES - 1
    vec = _const_spec((1, D_MODEL))
    return pl.pallas_call(
        functools.partial(_ffn_kernel, alpha=alpha),
        grid=(b, n // tm),
        in_specs=[pl.BlockSpec((None, tm, D_MODEL), lambda bi, i: (bi, i, 0)),
                  pl.BlockSpec((None, SUBLANES, D_MODEL),
                               lambda bi, i: (bi, jnp.maximum(i * per - 1, 0), 0)),
                  pl.BlockSpec((None, SUBLANES, D_MODEL),
                               lambda bi, i: (bi, jnp.minimum((i + 1) * per, last), 0)),
                  _const_spec((D_MODEL, 2 * D_FF)), _const_spec((3, D_FF)), _const_spec((1, D_FF)),
                  _const_spec((D_FF, D_MODEL)), vec, vec],
        out_specs=pl.BlockSpec((None, tm, D_MODEL), lambda bi, i: (bi, i, 0)),
        out_shape=jax.ShapeDtypeStruct((b, n, D_MODEL), F32),
        scratch_shapes=[pltpu.VMEM((tm, D_FF), BF16)],
        compiler_params=_params("parallel", "parallel"),
        name="ffn",
    )(x, x, x, w_up.astype(BF16), conv_w, conv_b[None, :], w_down.astype(BF16),
      ln_g[None, :], ln_b[None, :])


def _encoder(x, mem, w_in, q_norm_g, k_norm_g, out_norm_a, out_norm_b, w_o, ln1_g, ln1_b,
             wc_q, wc_k, wc_v, wc_o, ln2_g, ln2_b, w_up, conv_w, conv_b, w_down, ln3_g, ln3_b):
    depth = w_in.shape[0]
    alpha = (2 * depth) ** 0.25
    for l in range(depth):
        qa, ka, va, qb, kb, vb = _project(x, w_in[l], q_norm_g[l], k_norm_g[l])
        oa = _dilated_attention(qa, ka, va)
        ob = _gqa_attention(qb, kb, vb)
        kmem, vmem = _memory_kv(mem, wc_k[l], wc_v[l])
        x = _mix_cross(x, oa, ob, kmem, vmem, out_norm_a[l], out_norm_b[l], w_o[l],
                       ln1_g[l], ln1_b[l], wc_q[l], wc_o[l], ln2_g[l], ln2_b[l], alpha)
        x = _conv_glu_ffn(x, w_up[l], conv_w[l], conv_b[l], w_down[l], ln3_g[l], ln3_b[l], alpha)
    return x


def kernel(x_prompt, x_sample, mem_prompt, mem_sample, w_in, q_norm_g, k_norm_g, out_norm_a, out_norm_b, w_o, ln1_g, ln1_b, wc_q, wc_k, wc_v, wc_o, ln2_g, ln2_b, w_up, conv_w, conv_b, w_down, ln3_g, ln3_b):
    weights = (w_in, q_norm_g, k_norm_g, out_norm_a, out_norm_b, w_o, ln1_g, ln1_b,
               wc_q, wc_k, wc_v, wc_o, ln2_g, ln2_b, w_up, conv_w, conv_b, w_down, ln3_g, ln3_b)
    return (_encoder(x_prompt, mem_prompt, *weights), _encoder(x_sample, mem_sample, *weights))
```

```python
import functools

import jax
import jax.numpy as jnp
from jax import lax
from jax.experimental import pallas as pl
from jax.experimental.pallas import tpu as pltpu

D_MODEL = 1024
HEAD_DIM = 64
N_HEADS_A = 8
N_HEADS_B = 8
N_KV_B = 2
WIDTH_A = N_HEADS_A * HEAD_DIM
WIDTH_B = N_HEADS_B * HEAD_DIM
KV_WIDTH_B = N_KV_B * HEAD_DIM
DILATED_PATTERNS = ((128, 1), (512, 4), (2048, 16))
N_HEADS_MEM = 4
HEAD_DIM_MEM = D_MODEL // N_HEADS_MEM
D_FF = 2816
GRID_W = 64
ROPE_THETA = 10000.0
LN_EPS = 1e-5
RMS_EPS = 1e-6
LOG2_E = 1.4426950408889634

LANES = 128
SUBLANES = 8
VMEM_LIMIT_BYTES = 56 * 1024 * 1024

ROW_TILE = 512
BAND_Q = 128
BAND_UNROLL = 16
GQA_Q = 512
GQA_UNROLL = 8
FF_CHUNK = 256

F32 = jnp.float32
BF16 = jnp.bfloat16


def _params(*semantics):
    return pltpu.CompilerParams(dimension_semantics=semantics, vmem_limit_bytes=VMEM_LIMIT_BYTES)


def _const_spec(shape):
    zeros = (0,) * len(shape)
    return pl.BlockSpec(shape, lambda *_: zeros, pipeline_mode=pl.Buffered(1))


def _swap_within(t, half, first):
    n = t.shape[-1]
    fwd = pltpu.roll(t, n - half, axis=1)
    bwd = pltpu.roll(t, half, axis=1)
    return jnp.where(first, fwd, bwd)


def _dot(a, b):
    return jnp.dot(a, b, preferred_element_type=F32)


def _dot_nt(a, b):
    return lax.dot_general(a, b, (((1,), (1,)), ((), ())), preferred_element_type=F32)


def _proj_kernel(x_ref, w_ref, ca_ref, sa_ref, cb_ref, sb_ref, gq_ref, gk_ref, gsum_ref,
                 qa_ref, ka_ref, va_ref, qb_ref, kb_ref, vt_ref):
    rows = x_ref.shape[0]
    xb = x_ref[...].astype(BF16)
    lane = lax.broadcasted_iota(jnp.int32, (rows, LANES), 1)
    first32 = (lane % 64) < 32
    first16 = (lane % 32) < 16
    low = lane < 64
    ca, sa = ca_ref[...], sa_ref[...]
    cb, sb = cb_ref[...], sb_ref[...]
    gsum = gsum_ref[...]

    def rope_a(t, scale):
        return (t * ca + _swap_within(t, 32, first32) * sa) * scale

    def head_rms(t, g):
        sq = t * t
        hi = sq.astype(BF16)
        lo = (sq - hi.astype(F32)).astype(BF16)
        ss = _dot(hi, gsum) + _dot(lo, gsum)
        return t * lax.rsqrt(ss * (1.0 / HEAD_DIM) + RMS_EPS) * g

    def rope_b(t, scale):
        return (t * cb + _swap_within(t, 16, first16) * sb) * scale

    qscale = HEAD_DIM ** -0.5
    c0 = 0
    qa = _dot(xb, w_ref[:, c0:c0 + WIDTH_A]); c0 += WIDTH_A
    for c in range(WIDTH_A // LANES):
        sl = slice(c * LANES, (c + 1) * LANES)
        qa_ref[:, sl] = rope_a(qa[:, sl], qscale * LOG2_E)
    ka = _dot(xb, w_ref[:, c0:c0 + WIDTH_A]); c0 += WIDTH_A
    for c in range(WIDTH_A // LANES):
        sl = slice(c * LANES, (c + 1) * LANES)
        ka_ref[:, sl] = rope_a(ka[:, sl], 1.0)
    va_ref[...] = _dot(xb, w_ref[:, c0:c0 + WIDTH_A]); c0 += WIDTH_A

    qb = _dot(xb, w_ref[:, c0:c0 + WIDTH_B]); c0 += WIDTH_B
    gq = gq_ref[...]
    zero = jnp.zeros((rows, LANES), F32)
    for c in range(WIDTH_B // LANES):
        y = rope_b(head_rms(qb[:, c * LANES:(c + 1) * LANES], gq), qscale * LOG2_E)
        rolled = pltpu.roll(y, 64, axis=1)
        if (2 * c) // (N_HEADS_B // N_KV_B) == 0:
            even, odd = jnp.where(low, y, zero), jnp.where(low, rolled, zero)
        else:
            even, odd = jnp.where(low, zero, rolled), jnp.where(low, zero, y)
        qb_ref[:, (2 * c) * LANES:(2 * c + 1) * LANES] = even.astype(BF16)
        qb_ref[:, (2 * c + 1) * LANES:(2 * c + 2) * LANES] = odd.astype(BF16)

    kv = _dot(xb, w_ref[:, c0:c0 + 2 * KV_WIDTH_B])
    kb_ref[...] = rope_b(head_rms(kv[:, :KV_WIDTH_B], gk_ref[...]), 1.0).astype(BF16)
    vt = kv[:, KV_WIDTH_B:].T
    row = lax.broadcasted_iota(jnp.int32, vt.shape, 0)
    vt_ref[0] = jnp.where(row < HEAD_DIM, vt, 1.0).astype(BF16)
    vt_ref[1] = jnp.where(row < HEAD_DIM, 1.0, vt).astype(BF16)


def _rope_tables(n):
    pos = jnp.arange(n)

    def angles(p, dim):
        inv = ROPE_THETA ** (-jnp.arange(0, dim, 2, dtype=F32) / dim)
        ang = p.astype(F32)[:, None] * inv[None, :]
        return jnp.cos(ang), jnp.sin(ang)

    c, s = angles(pos, HEAD_DIM)
    ca = jnp.tile(jnp.concatenate([c, c], axis=-1), (1, 2))
    sa = jnp.tile(jnp.concatenate([-s, s], axis=-1), (1, 2))
    cr, sr = angles(pos // GRID_W, HEAD_DIM // 2)
    cc, sc = angles(pos % GRID_W, HEAD_DIM // 2)
    cb = jnp.tile(jnp.concatenate([cr, cr, cc, cc], axis=-1), (1, 2))
    sb = jnp.tile(jnp.concatenate([-sr, sr, -sc, sc], axis=-1), (1, 2))
    return ca, sa, cb, sb


def _project(x, w_in, q_norm_g, k_norm_g):
    b, n, _ = x.shape
    tm = min(ROW_TILE, n)
    ca, sa, cb, sb = _rope_tables(n)
    gq = jnp.tile(q_norm_g, 2)[None, :]
    gk = jnp.tile(k_norm_g, 2)[None, :]
    grp = jnp.arange(LANES) // HEAD_DIM
    gsum = (grp[:, None] == grp[None, :]).astype(BF16)
    in_cols = w_in.shape[1]
    tab = pl.BlockSpec((tm, LANES), lambda bi, i: (i, 0))

    def out(width):
        return pl.BlockSpec((None, tm, width), lambda bi, i: (bi, i, 0))

    return pl.pallas_call(
        _proj_kernel,
        grid=(b, n // tm),
        in_specs=[pl.BlockSpec((None, tm, D_MODEL), lambda bi, i: (bi, i, 0)),
                  _const_spec((D_MODEL, in_cols)), tab, tab, tab, tab,
                  _const_spec((1, LANES)), _const_spec((1, LANES)), _const_spec((LANES, LANES))],
        out_specs=[out(WIDTH_A), out(WIDTH_A), out(WIDTH_A), out(2 * WIDTH_B), out(KV_WIDTH_B),
                   pl.BlockSpec((None, N_KV_B, None, KV_WIDTH_B, tm), lambda bi, i: (bi, 0, i, 0, 0))],
        out_shape=[jax.ShapeDtypeStruct((b, n, WIDTH_A), F32)] * 3
        + [jax.ShapeDtypeStruct((b, n, 2 * WIDTH_B), BF16),
           jax.ShapeDtypeStruct((b, n, KV_WIDTH_B), BF16),
           jax.ShapeDtypeStruct((b, N_KV_B, n // tm, KV_WIDTH_B, tm), BF16)],
        compiler_params=_params("parallel", "parallel"),
        name="proj",
    )(x, w_in.astype(BF16), ca, sa, cb, sb, gq, gk, gsum)


def _dilated_kernel(q_ref, k_ref, v_ref, o_ref, num_ref, den_ref, max_ref, bias_ref, s_ref, top_ref):
    n = q_ref.shape[0]
    last = len(DILATED_PATTERNS) - 1
    for idx, (window, d) in enumerate(sorted(DILATED_PATTERNS, key=lambda wd: -wd[1])):
        half = window // (2 * d)
        length = n // d
        tq = min(BAND_Q, length)
        win = min(tq + 2 * half, length)
        per_phase = length // tq
        lane = lax.broadcasted_iota(jnp.int32, (tq, LANES), 1)
        low = lane < HEAD_DIM
        vlane = lax.broadcasted_iota(jnp.int32, (win, LANES), 1)
        vlow = jnp.where(vlane < HEAD_DIM, 1.0, 0.0).astype(BF16)
        vhigh = jnp.where(vlane < HEAD_DIM, 0.0, 1.0).astype(BF16)
        qi = lax.broadcasted_iota(jnp.int32, (tq, win), 0)
        kj = lax.broadcasted_iota(jnp.int32, (tq, win), 1)
        for case, off in enumerate((0, -half, tq - win)):
            bias_ref[case, :tq, :win] = jnp.where(jnp.abs(kj + off - qi) <= half, 0.0, -jnp.inf)

        total = d * per_phase

        def place(t, d=d, half=half, length=length, tq=tq, win=win, per_phase=per_phase):
            r = t // per_phase
            q0 = (t % per_phase) * tq
            k0 = jnp.clip(q0 - half, 0, length - win)
            case = jnp.where(q0 < half, 0, jnp.where(q0 - half > length - win, 2, 1))
            return pl.ds(r + d * q0, tq, stride=d), pl.ds(r + d * k0, win, stride=d), case

        def scores(t, slot, tq=tq, win=win, low=low, place=place):
            rows, keys, case = place(t)
            q = q_ref[rows, :].astype(BF16)
            k = k_ref[keys, :].astype(BF16)
            bias = bias_ref[case, :tq, :win]
            for h, own in enumerate((low, jnp.logical_not(low))):
                s = _dot_nt(jnp.where(own, q, jnp.zeros_like(q)), k) + bias
                s_ref[slot, h, :tq, :win] = s
                top_ref[slot, h, :tq, :] = jnp.broadcast_to(jnp.max(s, axis=-1, keepdims=True), (tq, LANES))

        def finish(t, slot, tq=tq, win=win, low=low, vlow=vlow, vhigh=vhigh, place=place, idx=idx):
            rows, keys, _ = place(t)
            v = v_ref[keys, :].astype(BF16)
            wide = tuple(jnp.concatenate([v * own, own], axis=1) for own in (vlow, vhigh))
            both = None
            for h in range(2):
                top_wide = jnp.tile(top_ref[slot, h, :tq, :], (1, win // LANES))
                part = _dot(jnp.exp2(s_ref[slot, h, :tq, :win] - top_wide).astype(BF16), wide[h])
                both = part if both is None else both + part
            num, den = both[:, :LANES], both[:, LANES:]
            top = jnp.where(low, top_ref[slot, 0, :tq, :], top_ref[slot, 1, :tq, :])
            if idx > 0:
                old_top = max_ref[rows, :]
                new_top = jnp.maximum(old_top, top)
                a_old, a_new = jnp.exp2(old_top - new_top), jnp.exp2(top - new_top)
                num = num_ref[rows, :] * a_old + num * a_new
                den = den_ref[rows, :] * a_old + den * a_new
                top = new_top
            if idx < last:
                num_ref[rows, :] = num
                den_ref[rows, :] = den
                max_ref[rows, :] = top
            else:
                num_ref[rows, :] = num / den

        unroll = min(BAND_UNROLL, total)

        def trip(i, carry, scores=scores, finish=finish, total=total, unroll=unroll):
            for u in range(unroll):
                t = i * unroll + u
                scores(jnp.minimum(t + 1, total - 1), (u + 1) % 2)
                finish(t, u % 2)
            return carry

        scores(0, 0)
        lax.fori_loop(0, total // unroll, trip, 0)
    o_ref[...] = num_ref[...].astype(o_ref.dtype)


def _dilated_attention(qa, ka, va):
    b, n, width = qa.shape
    spec = pl.BlockSpec((None, n, LANES), lambda bi, h: (bi, 0, h))
    tq = max(min(BAND_Q, n // d) for _, d in DILATED_PATTERNS)
    win = max(min(min(BAND_Q, n // d) + w // d, n // d) for w, d in DILATED_PATTERNS)
    assert all((n // d // min(BAND_Q, n // d)) * d % 2 == 0 for _, d in DILATED_PATTERNS)
    return pl.pallas_call(
        _dilated_kernel,
        grid=(b, width // LANES),
        in_specs=[spec, spec, spec],
        out_specs=spec,
        out_shape=jax.ShapeDtypeStruct((b, n, width), BF16),
        scratch_shapes=[pltpu.VMEM((n, LANES), F32)] * 3 + [pltpu.VMEM((3, tq, win), F32),
                        pltpu.VMEM((2, 2, tq, win), F32), pltpu.VMEM((2, 2, tq, LANES), F32)],
        compiler_params=_params("parallel", "parallel"),
        name="dilated",
    )(qa, ka, va)


def _gqa_kernel(q_ref, k_ref, vt_ref, o_ref, s_ref):
    tq = q_ref.shape[0]
    chunks, _, tk = vt_ref.shape
    qs = (q_ref[:, :LANES], q_ref[:, LANES:])

    def scores(c, slot):
        k = k_ref[pl.ds(pl.multiple_of(c * tk, tk), tk), :]
        tops = []
        for h, q in enumerate(qs):
            s = _dot_nt(k, q)
            s_ref[slot, h] = s
            tops.append(jnp.max(s, axis=0, keepdims=True))
        return tuple(tops)

    def accumulate(c, slot, tops, state):
        vt = vt_ref[c]
        new = []
        for h, (m, acc) in enumerate(state):
            m_new = jnp.maximum(m, tops[h])
            p = jnp.exp2(s_ref[slot, h] - m_new).astype(BF16)
            new.append((m_new, jnp.exp2(m - m_new) * acc + _dot(vt, p)))
        return tuple(new)

    unroll = min(GQA_UNROLL, chunks)

    def trip(t, carry):
        tops, state = carry
        for u in range(unroll):
            c = t * unroll + u
            nxt = scores(jnp.minimum(c + 1, chunks - 1), (u + 1) % 2)
            state = accumulate(c, u % 2, tops, state)
            tops = nxt
        return tops, state

    state = ((jnp.full((1, tq), -jnp.inf, F32), jnp.zeros((LANES, tq), F32)),) * 2
    _, carry = lax.fori_loop(0, chunks // unroll, trip, (scores(0, 0), state))
    kv0 = pl.program_id(2) < (N_HEADS_B // N_KV_B) // 2
    outs = []
    for _, acc in carry:
        num = jnp.where(kv0, acc[:HEAD_DIM], acc[HEAD_DIM:])
        den = jnp.where(kv0, acc[HEAD_DIM:HEAD_DIM + 1], acc[0:1])
        outs.append(num / den)
    o_ref[...] = jnp.concatenate(outs, axis=0).T.astype(o_ref.dtype)


def _gqa_attention(qb, kb, vt):
    b, n, _ = qb.shape
    _, _, chunks, _, tk = vt.shape
    assert chunks % 2 == 0 and chunks % min(GQA_UNROLL, chunks) == 0, "slots alternate by chunk parity"
    tq = min(GQA_Q, n)
    pairs_per_kv = (N_HEADS_B // N_KV_B) // 2
    return pl.pallas_call(
        _gqa_kernel,
        grid=(b, n // tq, WIDTH_B // LANES),
        in_specs=[pl.BlockSpec((None, tq, 2 * LANES), lambda bi, i, j: (bi, i, j)),
                  pl.BlockSpec((None, n, KV_WIDTH_B), lambda bi, i, j: (bi, 0, 0)),
                  pl.BlockSpec((None, None, chunks, KV_WIDTH_B, tk),
                               lambda bi, i, j: (bi, j // pairs_per_kv, 0, 0, 0))],
        out_specs=pl.BlockSpec((None, tq, LANES), lambda bi, i, j: (bi, i, j)),
        out_shape=jax.ShapeDtypeStruct((b, n, WIDTH_B), BF16),
        scratch_shapes=[pltpu.VMEM((2, 2, tk, tq), F32)],
        compiler_params=_params("parallel", "parallel", "parallel"),
        name="gqa",
    )(qb, kb, vt)


def _memkv_kernel(mem_ref, wk_ref, wv_ref, k_ref, v_ref):
    mb = mem_ref[...].astype(BF16)
    k_ref[...] = _dot(mb, wk_ref[...]).astype(BF16)
    v_ref[...] = _dot(mb, wv_ref[...]).astype(BF16)


def _memory_kv(mem, wk, wv):
    b, m, _ = mem.shape
    spec = pl.BlockSpec((None, m, D_MODEL), lambda bi: (bi, 0, 0))
    wspec = _const_spec((D_MODEL, D_MODEL))
    return pl.pallas_call(
        _memkv_kernel,
        grid=(b,),
        in_specs=[spec, wspec, wspec],
        out_specs=[spec, spec],
        out_shape=[jax.ShapeDtypeStruct((b, m, D_MODEL), BF16)] * 2,
        compiler_params=_params("parallel"),
        name="memkv",
    )(mem, wk.astype(BF16), wv.astype(BF16))


def _layer_norm(x, g, b):
    mu = jnp.mean(x, axis=-1, keepdims=True)
    xc = x - mu
    var = jnp.mean(xc * xc, axis=-1, keepdims=True)
    return xc * lax.rsqrt(var + LN_EPS) * g + b


def _rms_norm(x, g):
    ms = jnp.mean(x * x, axis=-1, keepdims=True)
    return x * lax.rsqrt(ms + RMS_EPS) * g


def _mix_kernel(x_ref, oa_ref, ob_ref,
                ga_ref, gb_ref, wo_ref, g1_ref, b1_ref, wq_ref, km_ref, vm_ref, wco_ref,
                g2_ref, b2_ref, y_ref, *, alpha):
    oa = _rms_norm(oa_ref[...].astype(F32), ga_ref[...]).astype(BF16)
    ob = _rms_norm(ob_ref[...].astype(F32), gb_ref[...]).astype(BF16)
    mix = _dot(oa, wo_ref[:WIDTH_A, :]) + _dot(ob, wo_ref[WIDTH_A:, :])
    x1 = _layer_norm(alpha * x_ref[...] + mix, g1_ref[...], b1_ref[...])

    q = (_dot(x1.astype(BF16), wq_ref[...]) * (HEAD_DIM_MEM ** -0.5)).astype(BF16)
    heads = []
    for h in range(N_HEADS_MEM):
        sl = slice(h * HEAD_DIM_MEM, (h + 1) * HEAD_DIM_MEM)
        s = _dot_nt(q[:, sl], km_ref[:, sl])
        e = jnp.exp(s - jnp.max(s, axis=-1, keepdims=True))
        den = jnp.sum(e, axis=-1, keepdims=True)
        heads.append((_dot(e.astype(BF16), vm_ref[:, sl]) / den).astype(BF16))
    cross = _dot(jnp.concatenate(heads, axis=-1), wco_ref[...])
    y_ref[...] = _layer_norm(alpha * x1 + cross, g2_ref[...], b2_ref[...])


def _mix_cross(x, oa, ob, kmem, vmem, out_norm_a, out_norm_b, w_o, ln1_g, ln1_b,
               wc_q, wc_o, ln2_g, ln2_b, alpha):
    b, n, _ = x.shape
    tm = min(ROW_TILE, n)
    m = kmem.shape[1]

    def rows(width):
        return pl.BlockSpec((None, tm, width), lambda bi, i: (bi, i, 0))

    vec = lambda width: _const_spec((1, width))
    memspec = pl.BlockSpec((None, m, D_MODEL), lambda bi, i: (bi, 0, 0))
    wspec = _const_spec((D_MODEL, D_MODEL))
    return pl.pallas_call(
        functools.partial(_mix_kernel, alpha=alpha),
        grid=(b, n // tm),
        in_specs=[rows(D_MODEL), rows(WIDTH_A), rows(WIDTH_B), vec(WIDTH_A), vec(WIDTH_B),
                  wspec, vec(D_MODEL), vec(D_MODEL), wspec, memspec, memspec, wspec,
                  vec(D_MODEL), vec(D_MODEL)],
        out_specs=rows(D_MODEL),
        out_shape=jax.ShapeDtypeStruct((b, n, D_MODEL), F32),
        compiler_params=_params("parallel", "parallel"),
        name="mix",
    )(x, oa, ob, out_norm_a[None, :], out_norm_b[None, :], w_o.astype(BF16),
      ln1_g[None, :], ln1_b[None, :], wc_q.astype(BF16), kmem, vmem, wc_o.astype(BF16),
      ln2_g[None, :], ln2_b[None, :])


def _ffn_kernel(x_ref, prev_ref, next_ref, wup_ref, cw_ref, cb_ref, wdn_ref, g_ref, b_ref,
                y_ref, h_ref, *, alpha):
    rows = x_ref.shape[0]
    i = pl.program_id(1)
    x = x_ref[...]
    prev = jnp.where(i > 0, prev_ref[...], 0.0)
    nxt = jnp.where(i < pl.num_programs(1) - 1, next_ref[...], 0.0)
    xb = x.astype(BF16)
    xe = jnp.concatenate([prev, x, nxt], axis=0).astype(BF16)
    ext = rows + 2 * SUBLANES
    for c in range(D_FF // FF_CHUNK):
        sl = slice(c * FF_CHUNK, (c + 1) * FF_CHUNK)
        gate = _dot(xe, wup_ref[:, sl])
        before = pltpu.roll(gate, 1, axis=0)[SUBLANES:SUBLANES + rows]
        after = pltpu.roll(gate, ext - 1, axis=0)[SUBLANES:SUBLANES + rows]
        here = gate[SUBLANES:SUBLANES + rows]
        conv = (before * cw_ref[0:1, sl] + here * cw_ref[1:2, sl] + after * cw_ref[2:3, sl]
                + cb_ref[:, sl])
        act = 0.5 * conv * (1.0 + lax.erf(conv * (2.0 ** -0.5)))
        val = _dot(xb, wup_ref[:, D_FF + c * FF_CHUNK:D_FF + (c + 1) * FF_CHUNK])
        h_ref[:, sl] = (act * val).astype(BF16)
    y = _dot(h_ref[...], wdn_ref[...])
    y_ref[...] = _layer_norm(alpha * x + y, g_ref[...], b_ref[...])


def _conv_glu_ffn(x, w_up, conv_w, conv_b, w_down, ln_g, ln_b, alpha):
    b, n, _ = x.shape
    tm = min(ROW_TILE, n)
    per = tm // SUBLANES
    last = n // SUBLANES - 1
    vec = _const_spec((1, D_MODEL))
    return pl.pallas_call(
        functools.partial(_ffn_kernel, alpha=alpha),
        grid=(b, n // tm),
        in_specs=[pl.BlockSpec((None, tm, D_MODEL), lambda bi, i: (bi, i, 0)),
                  pl.BlockSpec((None, SUBLANES, D_MODEL),
                               lambda bi, i: (bi, jnp.maximum(i * per - 1, 0), 0)),
                  pl.BlockSpec((None, SUBLANES, D_MODEL),
                               lambda bi, i: (bi, jnp.minimum((i + 1) * per, last), 0)),
                  _const_spec((D_MODEL, 2 * D_FF)), _const_spec((3, D_FF)), _const_spec((1, D_FF)),
                  _const_spec((D_FF, D_MODEL)), vec, vec],
        out_specs=pl.BlockSpec((None, tm, D_MODEL), lambda bi, i: (bi, i, 0)),
        out_shape=jax.ShapeDtypeStruct((b, n, D_MODEL), F32),
        scratch_shapes=[pltpu.VMEM((tm, D_FF), BF16)],
        compiler_params=_params("parallel", "parallel"),
        name="ffn",
    )(x, x, x, w_up.astype(BF16), conv_w, conv_b[None, :], w_down.astype(BF16),
      ln_g[None, :], ln_b[None, :])


def _encoder(x, mem, w_in, q_norm_g, k_norm_g, out_norm_a, out_norm_b, w_o, ln1_g, ln1_b,
             wc_q, wc_k, wc_v, wc_o, ln2_g, ln2_b, w_up, conv_w, conv_b, w_down, ln3_g, ln3_b):
    depth = w_in.shape[0]
    alpha = (2 * depth) ** 0.25
    for l in range(depth):
        qa, ka, va, qb, kb, vb = _project(x, w_in[l], q_norm_g[l], k_norm_g[l])
        oa = _dilated_attention(qa, ka, va)
        ob = _gqa_attention(qb, kb, vb)
        kmem, vmem = _memory_kv(mem, wc_k[l], wc_v[l])
        x = _mix_cross(x, oa, ob, kmem, vmem, out_norm_a[l], out_norm_b[l], w_o[l],
                       ln1_g[l], ln1_b[l], wc_q[l], wc_o[l], ln2_g[l], ln2_b[l], alpha)
        x = _conv_glu_ffn(x, w_up[l], conv_w[l], conv_b[l], w_down[l], ln3_g[l], ln3_b[l], alpha)
    return x


def kernel(x_prompt, x_sample, mem_prompt, mem_sample, w_in, q_norm_g, k_norm_g, out_norm_a, out_norm_b, w_o, ln1_g, ln1_b, wc_q, wc_k, wc_v, wc_o, ln2_g, ln2_b, w_up, conv_w, conv_b, w_down, ln3_g, ln3_b):
    weights = (w_in, q_norm_g, k_norm_g, out_norm_a, out_norm_b, w_o, ln1_g, ln1_b,
               wc_q, wc_k, wc_v, wc_o, ln2_g, ln2_b, w_up, conv_w, conv_b, w_down, ln3_g, ln3_b)
    return (_encoder(x_prompt, mem_prompt, *weights), _encoder(x_sample, mem_sample, *weights))
```

```python
import functools

import jax
import jax.numpy as jnp
from jax import lax
from jax.experimental import pallas as pl
from jax.experimental.pallas import tpu as pltpu

D_MODEL = 1024
HEAD_DIM = 64
N_HEADS_A = 8
N_HEADS_B = 8
N_KV_B = 2
WIDTH_A = N_HEADS_A * HEAD_DIM
WIDTH_B = N_HEADS_B * HEAD_DIM
KV_WIDTH_B = N_KV_B * HEAD_DIM
DILATED_PATTERNS = ((128, 1), (512, 4), (2048, 16))
N_HEADS_MEM = 4
HEAD_DIM_MEM = D_MODEL // N_HEADS_MEM
D_FF = 2816
GRID_W = 64
ROPE_THETA = 10000.0
LN_EPS = 1e-5
RMS_EPS = 1e-6
LOG2_E = 1.4426950408889634

LANES = 128
SUBLANES = 8
VMEM_LIMIT_BYTES = 56 * 1024 * 1024

ROW_TILE = 512
BAND_Q = 128
BAND_UNROLL = 16
GQA_Q = 512
GQA_UNROLL = 8
FF_CHUNK = 256

F32 = jnp.float32
BF16 = jnp.bfloat16


def _params(*semantics):
    return pltpu.CompilerParams(dimension_semantics=semantics, vmem_limit_bytes=VMEM_LIMIT_BYTES)


def _const_spec(shape):
    zeros = (0,) * len(shape)
    return pl.BlockSpec(shape, lambda *_: zeros, pipeline_mode=pl.Buffered(1))


def _swap_within(t, half, first):
    n = t.shape[-1]
    fwd = pltpu.roll(t, n - half, axis=1)
    bwd = pltpu.roll(t, half, axis=1)
    return jnp.where(first, fwd, bwd)


def _dot(a, b):
    return jnp.dot(a, b, preferred_element_type=F32)


def _dot_nt(a, b):
    return lax.dot_general(a, b, (((1,), (1,)), ((), ())), preferred_element_type=F32)


def _proj_kernel(x_ref, w_ref, ca_ref, sa_ref, cb_ref, sb_ref, gq_ref, gk_ref, gsum_ref,
                 qa_ref, ka_ref, va_ref, qb_ref, kb_ref, vt_ref):
    rows = x_ref.shape[0]
    xb = x_ref[...].astype(BF16)
    lane = lax.broadcasted_iota(jnp.int32, (rows, LANES), 1)
    first32 = (lane % 64) < 32
    first16 = (lane % 32) < 16
    low = lane < 64
    ca, sa = ca_ref[...], sa_ref[...]
    cb, sb = cb_ref[...], sb_ref[...]
    gsum = gsum_ref[...]

    def rope_a(t, scale):
        return (t * ca + _swap_within(t, 32, first32) * sa) * scale

    def head_rms(t, g):
        sq = t * t
        hi = sq.astype(BF16)
        lo = (sq - hi.astype(F32)).astype(BF16)
        ss = _dot(hi, gsum) + _dot(lo, gsum)
        return t * lax.rsqrt(ss * (1.0 / HEAD_DIM) + RMS_EPS) * g

    def rope_b(t, scale):
        return (t * cb + _swap_within(t, 16, first16) * sb) * scale

    qscale = HEAD_DIM ** -0.5
    c0 = 0
    qa = _dot(xb, w_ref[:, c0:c0 + WIDTH_A]); c0 += WIDTH_A
    for c in range(WIDTH_A // LANES):
        sl = slice(c * LANES, (c + 1) * LANES)
        qa_ref[:, sl] = rope_a(qa[:, sl], qscale * LOG2_E)
    ka = _dot(xb, w_ref[:, c0:c0 + WIDTH_A]); c0 += WIDTH_A
    for c in range(WIDTH_A // LANES):
        sl = slice(c * LANES, (c + 1) * LANES)
        ka_ref[:, sl] = rope_a(ka[:, sl], 1.0)
    va_ref[...] = _dot(xb, w_ref[:, c0:c0 + WIDTH_A]); c0 += WIDTH_A

    qb = _dot(xb, w_ref[:, c0:c0 + WIDTH_B]); c0 += WIDTH_B
    gq = gq_ref[...]
    zero = jnp.zeros((rows, LANES), F32)
    for c in range(WIDTH_B // LANES):
        y = rope_b(head_rms(qb[:, c * LANES:(c + 1) * LANES], gq), qscale * LOG2_E)
        rolled = pltpu.roll(y, 64, axis=1)
        if (2 * c) // (N_HEADS_B // N_KV_B) == 0:
            even, odd = jnp.where(low, y, zero), jnp.where(low, rolled, zero)
        else:
            even, odd = jnp.where(low, zero, rolled), jnp.where(low, zero, y)
        qb_ref[:, (2 * c) * LANES:(2 * c + 1) * LANES] = even.astype(BF16)
        qb_ref[:, (2 * c + 1) * LANES:(2 * c + 2) * LANES] = odd.astype(BF16)

    kv = _dot(xb, w_ref[:, c0:c0 + 2 * KV_WIDTH_B])
    kb_ref[...] = rope_b(head_rms(kv[:, :KV_WIDTH_B], gk_ref[...]), 1.0).astype(BF16)
    vt = kv[:, KV_WIDTH_B:].T
    row = lax.broadcasted_iota(jnp.int32, vt.shape, 0)
    vt_ref[0] = jnp.where(row < HEAD_DIM, vt, 1.0).astype(BF16)
    vt_ref[1] = jnp.where(row < HEAD_DIM, 1.0, vt).astype(BF16)


def _rope_tables(n):
    pos = jnp.arange(n)

    def angles(p, dim):
        inv = ROPE_THETA ** (-jnp.arange(0, dim, 2, dtype=F32) / dim)
        ang = p.astype(F32)[:, None] * inv[None, :]
        return jnp.cos(ang), jnp.sin(ang)

    c, s = angles(pos, HEAD_DIM)
    ca = jnp.tile(jnp.concatenate([c, c], axis=-1), (1, 2))
    sa = jnp.tile(jnp.concatenate([-s, s], axis=-1), (1, 2))
    cr, sr = angles(pos // GRID_W, HEAD_DIM // 2)
    cc, sc = angles(pos % GRID_W, HEAD_DIM // 2)
    cb = jnp.tile(jnp.concatenate([cr, cr, cc, cc], axis=-1), (1, 2))
    sb = jnp.tile(jnp.concatenate([-sr, sr, -sc, sc], axis=-1), (1, 2))
    return ca, sa, cb, sb


def _project(x, w_in, q_norm_g, k_norm_g):
    b, n, _ = x.shape
    tm = min(ROW_TILE, n)
    ca, sa, cb, sb = _rope_tables(n)
    gq = jnp.tile(q_norm_g, 2)[None, :]
    gk = jnp.tile(k_norm_g, 2)[None, :]
    grp = jnp.arange(LANES) // HEAD_DIM
    gsum = (grp[:, None] == grp[None, :]).astype(BF16)
    in_cols = w_in.shape[1]
    tab = pl.BlockSpec((tm, LANES), lambda bi, i: (i, 0))

    def out(width):
        return pl.BlockSpec((None, tm, width), lambda bi, i: (bi, i, 0))

    return pl.pallas_call(
        _proj_kernel,
        grid=(b, n // tm),
        in_specs=[pl.BlockSpec((None, tm, D_MODEL), lambda bi, i: (bi, i, 0)),
                  _const_spec((D_MODEL, in_cols)), tab, tab, tab, tab,
                  _const_spec((1, LANES)), _const_spec((1, LANES)), _const_spec((LANES, LANES))],
        out_specs=[out(WIDTH_A), out(WIDTH_A), out(WIDTH_A), out(2 * WIDTH_B), out(KV_WIDTH_B),
                   pl.BlockSpec((None, N_KV_B, None, KV_WIDTH_B, tm), lambda bi, i: (bi, 0, i, 0, 0))],
        out_shape=[jax.ShapeDtypeStruct((b, n, WIDTH_A), F32)] * 3
        + [jax.ShapeDtypeStruct((b, n, 2 * WIDTH_B), BF16),
           jax.ShapeDtypeStruct((b, n, KV_WIDTH_B), BF16),
           jax.ShapeDtypeStruct((b, N_KV_B, n // tm, KV_WIDTH_B, tm), BF16)],
        compiler_params=_params("parallel", "parallel"),
        name="proj",
    )(x, w_in.astype(BF16), ca, sa, cb, sb, gq, gk, gsum)


def _dilated_kernel(q_ref, k_ref, v_ref, o_ref, num_ref, den_ref, max_ref, bias_ref, s_ref, top_ref):
    n = q_ref.shape[0]
    last = len(DILATED_PATTERNS) - 1
    for idx, (window, d) in enumerate(sorted(DILATED_PATTERNS, key=lambda wd: -wd[1])):
        half = window // (2 * d)
        length = n // d
        tq = min(BAND_Q, length)
        win = min(tq + 2 * half, length)
        per_phase = length // tq
        lane = lax.broadcasted_iota(jnp.int32, (tq, LANES), 1)
        low = lane < HEAD_DIM
        vlane = lax.broadcasted_iota(jnp.int32, (win, LANES), 1)
        vlow = jnp.where(vlane < HEAD_DIM, 1.0, 0.0).astype(BF16)
        vhigh = jnp.where(vlane < HEAD_DIM, 0.0, 1.0).astype(BF16)
        qi = lax.broadcasted_iota(jnp.int32, (tq, win), 0)
        kj = lax.broadcasted_iota(jnp.int32, (tq, win), 1)
        for case, off in enumerate((0, -half, tq - win)):
            bias_ref[case, :tq, :win] = jnp.where(jnp.abs(kj + off - qi) <= half, 0.0, -jnp.inf)

        total = d * per_phase

        def place(t, d=d, half=half, length=length, tq=tq, win=win, per_phase=per_phase):
            r = t // per_phase
            q0 = (t % per_phase) * tq
            k0 = jnp.clip(q0 - half, 0, length - win)
            case = jnp.where(q0 < half, 0, jnp.where(q0 - half > length - win, 2, 1))
            return pl.ds(r + d * q0, tq, stride=d), pl.ds(r + d * k0, win, stride=d), case

        def scores(t, slot, tq=tq, win=win, low=low, place=place):
            rows, keys, case = place(t)
            q = q_ref[rows, :].astype(BF16)
            k = k_ref[keys, :].astype(BF16)
            bias = bias_ref[case, :tq, :win]
            for h, own in enumerate((low, jnp.logical_not(low))):
                s = _dot_nt(jnp.where(own, q, jnp.zeros_like(q)), k) + bias
                s_ref[slot, h, :tq, :win] = s
                top_ref[slot, h, :tq, :] = jnp.broadcast_to(jnp.max(s, axis=-1, keepdims=True), (tq, LANES))

        def finish(t, slot, tq=tq, win=win, low=low, vlow=vlow, vhigh=vhigh, place=place, idx=idx):
            rows, keys, _ = place(t)
            v = v_ref[keys, :].astype(BF16)
            wide = tuple(jnp.concatenate([v * own, own], axis=1) for own in (vlow, vhigh))
            both = None
            for h in range(2):
                top_wide = jnp.tile(top_ref[slot, h, :tq, :], (1, win // LANES))
                part = _dot(jnp.exp2(s_ref[slot, h, :tq, :win] - top_wide).astype(BF16), wide[h])
                both = part if both is None else both + part
            num, den = both[:, :LANES], both[:, LANES:]
            top = jnp.where(low, top_ref[slot, 0, :tq, :], top_ref[slot, 1, :tq, :])
            if idx > 0:
                old_top = max_ref[rows, :]
                new_top = jnp.maximum(old_top, top)
                a_old, a_new = jnp.exp2(old_top - new_top), jnp.exp2(top - new_top)
                num = num_ref[rows, :] * a_old + num * a_new
                den = den_ref[rows, :] * a_old + den * a_new
                top = new_top
            if idx < last:
                num_ref[rows, :] = num
                den_ref[rows, :] = den
                max_ref[rows, :] = top
            else:
                num_ref[rows, :] = num / den

        unroll = min(BAND_UNROLL, total)

        def trip(i, carry, scores=scores, finish=finish, total=total, unroll=unroll):
            for u in range(unroll):
                t = i * unroll + u
                scores(jnp.minimum(t + 1, total - 1), (u + 1) % 2)
                finish(t, u % 2)
            return carry

        scores(0, 0)
        lax.fori_loop(0, total // unroll, trip, 0)
    o_ref[...] = num_ref[...].astype(o_ref.dtype)


def _dilated_attention(qa, ka, va):
    b, n, width = qa.shape
    spec = pl.BlockSpec((None, n, LANES), lambda bi, h: (bi, 0, h))
    tq = max(min(BAND_Q, n // d) for _, d in DILATED_PATTERNS)
    win = max(min(min(BAND_Q, n // d) + w // d, n // d) for w, d in DILATED_PATTERNS)
    assert all((n // d // min(BAND_Q, n // d)) * d % 2 == 0 for _, d in DILATED_PATTERNS)
    return pl.pallas_call(
        _dilated_kernel,
        grid=(b, width // LANES),
        in_specs=[spec, spec, spec],
        out_specs=spec,
        out_shape=jax.ShapeDtypeStruct((b, n, width), BF16),
        scratch_shapes=[pltpu.VMEM((n, LANES), F32)] * 3 + [pltpu.VMEM((3, tq, win), F32),
                        pltpu.VMEM((2, 2, tq, win), F32), pltpu.VMEM((2, 2, tq, LANES), F32)],
        compiler_params=_params("parallel", "parallel"),
        name="dilated",
    )(qa, ka, va)


def _gqa_kernel(q_ref, qn_ref, k_ref, vt_ref, o_ref, s_ref, top_ref):
    tq = q_ref.shape[0]
    chunks, _, tk = vt_ref.shape
    qs = (q_ref[:, :LANES], q_ref[:, LANES:])
    qn = (qn_ref[:, :LANES], qn_ref[:, LANES:])

    def scores(c, slot, queries=qs):
        start = c * tk if isinstance(c, int) else pl.multiple_of(c * tk, tk)
        k = k_ref[pl.ds(start, tk), :]
        tops = []
        for h, q in enumerate(queries):
            s = _dot_nt(k, q)
            s_ref[slot, h] = s
            tops.append(jnp.max(s, axis=0, keepdims=True))
        return tuple(tops)

    @pl.when(jnp.logical_and(pl.program_id(1) == 0, pl.program_id(2) == 0))
    def _():
        for h, top in enumerate(scores(0, 0)):
            top_ref[h] = top

    def accumulate(c, slot, tops, state):
        vt = vt_ref[c]
        new = []
        for h, (m, acc) in enumerate(state):
            m_new = jnp.maximum(m, tops[h])
            p = jnp.exp2(s_ref[slot, h] - m_new).astype(BF16)
            new.append((m_new, jnp.exp2(m - m_new) * acc + _dot(vt, p)))
        return tuple(new)

    unroll = min(GQA_UNROLL, chunks)

    def trip(t, carry):
        tops, state = carry
        for u in range(unroll):
            c = t * unroll + u
            if isinstance(c, int) and c == chunks - 1:
                nxt = scores(0, 0, qn)
            else:
                nxt = scores(c + 1, (u + 1) % 2)
            state = accumulate(c, u % 2, tops, state)
            tops = nxt
        return tops, state

    state = ((jnp.full((1, tq), -jnp.inf, F32), jnp.zeros((LANES, tq), F32)),) * 2
    trips = chunks // unroll
    carry = lax.fori_loop(0, trips - 1, trip, ((top_ref[0], top_ref[1]), state))
    tops, carry = trip(trips - 1, carry)
    for h, top in enumerate(tops):
        top_ref[h] = top
    kv0 = pl.program_id(2) < (N_HEADS_B // N_KV_B) // 2
    outs = []
    for _, acc in carry:
        num = jnp.where(kv0, acc[:HEAD_DIM], acc[HEAD_DIM:])
        den = jnp.where(kv0, acc[HEAD_DIM:HEAD_DIM + 1], acc[0:1])
        outs.append(num / den)
    o_ref[...] = jnp.concatenate(outs, axis=0).T.astype(o_ref.dtype)


def _gqa_attention(qb, kb, vt):
    b, n, _ = qb.shape
    _, _, chunks, _, tk = vt.shape
    assert chunks % 2 == 0 and chunks % min(GQA_UNROLL, chunks) == 0, "slots alternate by chunk parity"
    tq = min(GQA_Q, n)
    pairs_per_kv = (N_HEADS_B // N_KV_B) // 2
    ni, nj = n // tq, WIDTH_B // LANES

    def next_step(bi, i, j):
        t = jnp.minimum((bi * ni + i) * nj + j + 1, b * ni * nj - 1)
        return (t // (ni * nj), (t // nj) % ni, t % nj)

    return pl.pallas_call(
        _gqa_kernel,
        grid=(b, ni, nj),
        in_specs=[pl.BlockSpec((None, tq, 2 * LANES), lambda bi, i, j: (bi, i, j)),
                  pl.BlockSpec((None, tq, 2 * LANES), next_step),
                  pl.BlockSpec((None, n, KV_WIDTH_B), lambda bi, i, j: (bi, 0, 0)),
                  pl.BlockSpec((None, None, chunks, KV_WIDTH_B, tk),
                               lambda bi, i, j: (bi, j // pairs_per_kv, 0, 0, 0))],
        out_specs=pl.BlockSpec((None, tq, LANES), lambda bi, i, j: (bi, i, j)),
        out_shape=jax.ShapeDtypeStruct((b, n, WIDTH_B), BF16),
        scratch_shapes=[pltpu.VMEM((2, 2, tk, tq), F32), pltpu.VMEM((2, 1, tq), F32)],
        compiler_params=_params("arbitrary", "arbitrary", "arbitrary"),
        name="gqa",
    )(qb, qb, kb, vt)


def _memkv_kernel(mem_ref, wk_ref, wv_ref, k_ref, v_ref):
    mb = mem_ref[...].astype(BF16)
    k_ref[...] = _dot(mb, wk_ref[...]).astype(BF16)
    v_ref[...] = _dot(mb, wv_ref[...]).astype(BF16)


def _memory_kv(mem, wk, wv):
    b, m, _ = mem.shape
    spec = pl.BlockSpec((None, m, D_MODEL), lambda bi: (bi, 0, 0))
    wspec = _const_spec((D_MODEL, D_MODEL))
    return pl.pallas_call(
        _memkv_kernel,
        grid=(b,),
        in_specs=[spec, wspec, wspec],
        out_specs=[spec, spec],
        out_shape=[jax.ShapeDtypeStruct((b, m, D_MODEL), BF16)] * 2,
        compiler_params=_params("parallel"),
        name="memkv",
    )(mem, wk.astype(BF16), wv.astype(BF16))


def _layer_norm(x, g, b):
    mu = jnp.mean(x, axis=-1, keepdims=True)
    xc = x - mu
    var = jnp.mean(xc * xc, axis=-1, keepdims=True)
    return xc * lax.rsqrt(var + LN_EPS) * g + b


def _rms_norm(x, g):
    ms = jnp.mean(x * x, axis=-1, keepdims=True)
    return x * lax.rsqrt(ms + RMS_EPS) * g


def _mix_kernel(x_ref, oa_ref, ob_ref,
                ga_ref, gb_ref, wo_ref, g1_ref, b1_ref, wq_ref, km_ref, vm_ref, wco_ref,
                g2_ref, b2_ref, y_ref, *, alpha):
    oa = _rms_norm(oa_ref[...].astype(F32), ga_ref[...]).astype(BF16)
    ob = _rms_norm(ob_ref[...].astype(F32), gb_ref[...]).astype(BF16)
    mix = _dot(oa, wo_ref[:WIDTH_A, :]) + _dot(ob, wo_ref[WIDTH_A:, :])
    x1 = _layer_norm(alpha * x_ref[...] + mix, g1_ref[...], b1_ref[...])

    q = (_dot(x1.astype(BF16), wq_ref[...]) * (HEAD_DIM_MEM ** -0.5)).astype(BF16)
    heads = []
    for h in range(N_HEADS_MEM):
        sl = slice(h * HEAD_DIM_MEM, (h + 1) * HEAD_DIM_MEM)
        s = _dot_nt(q[:, sl], km_ref[:, sl])
        e = jnp.exp(s - jnp.max(s, axis=-1, keepdims=True))
        den = jnp.sum(e, axis=-1, keepdims=True)
        heads.append((_dot(e.astype(BF16), vm_ref[:, sl]) / den).astype(BF16))
    cross = _dot(jnp.concatenate(heads, axis=-1), wco_ref[...])
    y_ref[...] = _layer_norm(alpha * x1 + cross, g2_ref[...], b2_ref[...])


def _mix_cross(x, oa, ob, kmem, vmem, out_norm_a, out_norm_b, w_o, ln1_g, ln1_b,
               wc_q, wc_o, ln2_g, ln2_b, alpha):
    b, n, _ = x.shape
    tm = min(ROW_TILE, n)
    m = kmem.shape[1]

    def rows(width):
        return pl.BlockSpec((None, tm, width), lambda bi, i: (bi, i, 0))

    vec = lambda width: _const_spec((1, width))
    memspec = pl.BlockSpec((None, m, D_MODEL), lambda bi, i: (bi, 0, 0))
    wspec = _const_spec((D_MODEL, D_MODEL))
    return pl.pallas_call(
        functools.partial(_mix_kernel, alpha=alpha),
        grid=(b, n // tm),
        in_specs=[rows(D_MODEL), rows(WIDTH_A), rows(WIDTH_B), vec(WIDTH_A), vec(WIDTH_B),
                  wspec, vec(D_MODEL), vec(D_MODEL), wspec, memspec, memspec, wspec,
                  vec(D_MODEL), vec(D_MODEL)],
        out_specs=rows(D_MODEL),
        out_shape=jax.ShapeDtypeStruct((b, n, D_MODEL), F32),
        compiler_params=_params("parallel", "parallel"),
        name="mix",
    )(x, oa, ob, out_norm_a[None, :], out_norm_b[None, :], w_o.astype(BF16),
      ln1_g[None, :], ln1_b[None, :], wc_q.astype(BF16), kmem, vmem, wc_o.astype(BF16),
      ln2_g[None, :], ln2_b[None, :])


def _ffn_kernel(x_ref, prev_ref, next_ref, wup_ref, cw_ref, cb_ref, wdn_ref, g_ref, b_ref,
                y_ref, h_ref, *, alpha):
    rows = x_ref.shape[0]
    i = pl.program_id(1)
    x = x_ref[...]
    prev = jnp.where(i > 0, prev_ref[...], 0.0)
    nxt = jnp.where(i < pl.num_programs(1) - 1, next_ref[...], 0.0)
    xb = x.astype(BF16)
    xe = jnp.concatenate([prev, x, nxt], axis=0).astype(BF16)
    ext = rows + 2 * SUBLANES
    for c in range(D_FF // FF_CHUNK):
        sl = slice(c * FF_CHUNK, (c + 1) * FF_CHUNK)
        gate = _dot(xe, wup_ref[:, sl])
        before = pltpu.roll(gate, 1, axis=0)[SUBLANES:SUBLANES + rows]
        after = pltpu.roll(gate, ext - 1, axis=0)[SUBLANES:SUBLANES + rows]
        here = gate[SUBLANES:SUBLANES + rows]
        conv = (before * cw_ref[0:1, sl] + here * cw_ref[1:2, sl] + after * cw_ref[2:3, sl]
                + cb_ref[:, sl])
        act = 0.5 * conv * (1.0 + lax.erf(conv * (2.0 ** -0.5)))
        val = _dot(xb, wup_ref[:, D_FF + c * FF_CHUNK:D_FF + (c + 1) * FF_CHUNK])
        h_ref[:, sl] = (act * val).astype(BF16)
    y = _dot(h_ref[...], wdn_ref[...])
    y_ref[...] = _layer_norm(alpha * x + y, g_ref[...], b_ref[...])


def _conv_glu_ffn(x, w_up, conv_w, conv_b, w_down, ln_g, ln_b, alpha):
    b, n, _ = x.shape
    tm = min(ROW_TILE, n)
    per = tm // SUBLANES
    last = n // SUBLANES - 1
    vec = _const_spec((1, D_MODEL))
    return pl.pallas_call(
        functools.partial(_ffn_kernel, alpha=alpha),
        grid=(b, n // tm),
        in_specs=[pl.BlockSpec((None, tm, D_MODEL), lambda bi, i: (bi, i, 0)),
                  pl.BlockSpec((None, SUBLANES, D_MODEL),
                               lambda bi, i: (bi, jnp.maximum(i * per - 1, 0), 0)),
                  pl.BlockSpec((None, SUBLANES, D_MODEL),
                               lambda bi, i: (bi, jnp.minimum((i + 1) * per, last), 0)),
                  _const_spec((D_MODEL, 2 * D_FF)), _const_spec((3, D_FF)), _const_spec((1, D_FF)),
                  _const_spec((D_FF, D_MODEL)), vec, vec],
        out_specs=pl.BlockSpec((None, tm, D_MODEL), lambda bi, i: (bi, i, 0)),
        out_shape=jax.ShapeDtypeStruct((b, n, D_MODEL), F32),
        scratch_shapes=[pltpu.VMEM((tm, D_FF), BF16)],
        compiler_params=_params("parallel", "parallel"),
        name="ffn",
    )(x, x, x, w_up.astype(BF16), conv_w, conv_b[None, :], w_down.astype(BF16),
      ln_g[None, :], ln_b[None, :])


def _encoder(x, mem, w_in, q_norm_g, k_norm_g, out_norm_a, out_norm_b, w_o, ln1_g, ln1_b,
             wc_q, wc_k, wc_v, wc_o, ln2_g, ln2_b, w_up, conv_w, conv_b, w_down, ln3_g, ln3_b):
    depth = w_in.shape[0]
    alpha = (2 * depth) ** 0.25
    for l in range(depth):
        qa, ka, va, qb, kb, vb = _project(x, w_in[l], q_norm_g[l], k_norm_g[l])
        oa = _dilated_attention(qa, ka, va)
        ob = _gqa_attention(qb, kb, vb)
        kmem, vmem = _memory_kv(mem, wc_k[l], wc_v[l])
        x = _mix_cross(x, oa, ob, kmem, vmem, out_norm_a[l], out_norm_b[l], w_o[l],
                       ln1_g[l], ln1_b[l], wc_q[l], wc_o[l], ln2_g[l], ln2_b[l], alpha)
        x = _conv_glu_ffn(x, w_up[l], conv_w[l], conv_b[l], w_down[l], ln3_g[l], ln3_b[l], alpha)
    return x


def kernel(x_prompt, x_sample, mem_prompt, mem_sample, w_in, q_norm_g, k_norm_g, out_norm_a, out_norm_b, w_o, ln1_g, ln1_b, wc_q, wc_k, wc_v, wc_o, ln2_g, ln2_b, w_up, conv_w, conv_b, w_down, ln3_g, ln3_b):
    weights = (w_in, q_norm_g, k_norm_g, out_norm_a, out_norm_b, w_o, ln1_g, ln1_b,
               wc_q, wc_k, wc_v, wc_o, ln2_g, ln2_b, w_up, conv_w, conv_b, w_down, ln3_g, ln3_b)
    return (_encoder(x_prompt, mem_prompt, *weights), _encoder(x_sample, mem_sample, *weights))
```

```python
import functools

import jax
import jax.numpy as jnp
from jax import lax
from jax.experimental import pallas as pl
from jax.experimental.pallas import tpu as pltpu

D_MODEL = 1024
HEAD_DIM = 64
N_HEADS_A = 8
N_HEADS_B = 8
N_KV_B = 2
WIDTH_A = N_HEADS_A * HEAD_DIM
WIDTH_B = N_HEADS_B * HEAD_DIM
KV_WIDTH_B = N_KV_B * HEAD_DIM
DILATED_PATTERNS = ((128, 1), (512, 4), (2048, 16))
N_HEADS_MEM = 4
HEAD_DIM_MEM = D_MODEL // N_HEADS_MEM
D_FF = 2816
GRID_W = 64
ROPE_THETA = 10000.0
LN_EPS = 1e-5
RMS_EPS = 1e-6
LOG2_E = 1.4426950408889634

LANES = 128
SUBLANES = 8
VMEM_LIMIT_BYTES = 56 * 1024 * 1024

ROW_TILE = 512
BAND_Q = 128
BAND_UNROLL = 16
GQA_Q = 512
GQA_ONES_ROWS = 16
GQA_VT_ROWS = HEAD_DIM + GQA_ONES_ROWS
GQA_UNROLL = 16
FF_CHUNK = 256

F32 = jnp.float32
BF16 = jnp.bfloat16


def _params(*semantics):
    return pltpu.CompilerParams(dimension_semantics=semantics, vmem_limit_bytes=VMEM_LIMIT_BYTES)


def _const_spec(shape):
    zeros = (0,) * len(shape)
    return pl.BlockSpec(shape, lambda *_: zeros, pipeline_mode=pl.Buffered(1))


def _swap_within(t, half, first):
    n = t.shape[-1]
    fwd = pltpu.roll(t, n - half, axis=1)
    bwd = pltpu.roll(t, half, axis=1)
    return jnp.where(first, fwd, bwd)


def _dot(a, b):
    return jnp.dot(a, b, preferred_element_type=F32)


def _dot_nt(a, b):
    return lax.dot_general(a, b, (((1,), (1,)), ((), ())), preferred_element_type=F32)


def _proj_kernel(x_ref, w_ref, ca_ref, sa_ref, cb_ref, sb_ref, gq_ref, gk_ref, gsum_ref,
                 qa_ref, ka_ref, va_ref, qb_ref, kb_ref, vt_ref):
    rows = x_ref.shape[0]
    xb = x_ref[...].astype(BF16)
    lane = lax.broadcasted_iota(jnp.int32, (rows, LANES), 1)
    first32 = (lane % 64) < 32
    first16 = (lane % 32) < 16
    low = lane < 64
    ca, sa = ca_ref[...], sa_ref[...]
    cb, sb = cb_ref[...], sb_ref[...]
    gsum = gsum_ref[...]

    def rope_a(t, scale):
        return (t * ca + _swap_within(t, 32, first32) * sa) * scale

    def head_rms(t, g):
        sq = t * t
        hi = sq.astype(BF16)
        lo = (sq - hi.astype(F32)).astype(BF16)
        ss = _dot(hi, gsum) + _dot(lo, gsum)
        return t * lax.rsqrt(ss * (1.0 / HEAD_DIM) + RMS_EPS) * g

    def rope_b(t, scale):
        return (t * cb + _swap_within(t, 16, first16) * sb) * scale

    qscale = HEAD_DIM ** -0.5
    c0 = 0
    qa = _dot(xb, w_ref[:, c0:c0 + WIDTH_A]); c0 += WIDTH_A
    for c in range(WIDTH_A // LANES):
        sl = slice(c * LANES, (c + 1) * LANES)
        qa_ref[:, sl] = rope_a(qa[:, sl], qscale * LOG2_E)
    ka = _dot(xb, w_ref[:, c0:c0 + WIDTH_A]); c0 += WIDTH_A
    for c in range(WIDTH_A // LANES):
        sl = slice(c * LANES, (c + 1) * LANES)
        ka_ref[:, sl] = rope_a(ka[:, sl], 1.0)
    va_ref[...] = _dot(xb, w_ref[:, c0:c0 + WIDTH_A]); c0 += WIDTH_A

    qb = _dot(xb, w_ref[:, c0:c0 + WIDTH_B]); c0 += WIDTH_B
    gq = gq_ref[...]
    zero = jnp.zeros((rows, LANES), F32)
    for c in range(WIDTH_B // LANES):
        y = rope_b(head_rms(qb[:, c * LANES:(c + 1) * LANES], gq), qscale * LOG2_E)
        rolled = pltpu.roll(y, 64, axis=1)
        if (2 * c) // (N_HEADS_B // N_KV_B) == 0:
            even, odd = jnp.where(low, y, zero), jnp.where(low, rolled, zero)
        else:
            even, odd = jnp.where(low, zero, rolled), jnp.where(low, zero, y)
        qb_ref[:, (2 * c) * LANES:(2 * c + 1) * LANES] = even.astype(BF16)
        qb_ref[:, (2 * c + 1) * LANES:(2 * c + 2) * LANES] = odd.astype(BF16)

    kv = _dot(xb, w_ref[:, c0:c0 + 2 * KV_WIDTH_B])
    kb_ref[...] = rope_b(head_rms(kv[:, :KV_WIDTH_B], gk_ref[...]), 1.0).astype(BF16)
    vt = kv[:, KV_WIDTH_B:].T
    ones = jnp.ones((GQA_ONES_ROWS, rows), F32)
    for g in range(N_KV_B):
        vt_ref[g] = jnp.concatenate([vt[g * HEAD_DIM:(g + 1) * HEAD_DIM], ones], axis=0).astype(BF16)


def _rope_tables(n):
    pos = jnp.arange(n)

    def angles(p, dim):
        inv = ROPE_THETA ** (-jnp.arange(0, dim, 2, dtype=F32) / dim)
        ang = p.astype(F32)[:, None] * inv[None, :]
        return jnp.cos(ang), jnp.sin(ang)

    c, s = angles(pos, HEAD_DIM)
    ca = jnp.tile(jnp.concatenate([c, c], axis=-1), (1, 2))
    sa = jnp.tile(jnp.concatenate([-s, s], axis=-1), (1, 2))
    cr, sr = angles(pos // GRID_W, HEAD_DIM // 2)
    cc, sc = angles(pos % GRID_W, HEAD_DIM // 2)
    cb = jnp.tile(jnp.concatenate([cr, cr, cc, cc], axis=-1), (1, 2))
    sb = jnp.tile(jnp.concatenate([-sr, sr, -sc, sc], axis=-1), (1, 2))
    return ca, sa, cb, sb


def _project(x, w_in, q_norm_g, k_norm_g):
    b, n, _ = x.shape
    tm = min(ROW_TILE, n)
    ca, sa, cb, sb = _rope_tables(n)
    gq = jnp.tile(q_norm_g, 2)[None, :]
    gk = jnp.tile(k_norm_g, 2)[None, :]
    grp = jnp.arange(LANES) // HEAD_DIM
    gsum = (grp[:, None] == grp[None, :]).astype(BF16)
    in_cols = w_in.shape[1]
    tab = pl.BlockSpec((tm, LANES), lambda bi, i: (i, 0))

    def out(width):
        return pl.BlockSpec((None, tm, width), lambda bi, i: (bi, i, 0))

    return pl.pallas_call(
        _proj_kernel,
        grid=(b, n // tm),
        in_specs=[pl.BlockSpec((None, tm, D_MODEL), lambda bi, i: (bi, i, 0)),
                  _const_spec((D_MODEL, in_cols)), tab, tab, tab, tab,
                  _const_spec((1, LANES)), _const_spec((1, LANES)), _const_spec((LANES, LANES))],
        out_specs=[out(WIDTH_A), out(WIDTH_A), out(WIDTH_A), out(2 * WIDTH_B), out(KV_WIDTH_B),
                   pl.BlockSpec((None, N_KV_B, None, GQA_VT_ROWS, tm), lambda bi, i: (bi, 0, i, 0, 0))],
        out_shape=[jax.ShapeDtypeStruct((b, n, WIDTH_A), F32)] * 3
        + [jax.ShapeDtypeStruct((b, n, 2 * WIDTH_B), BF16),
           jax.ShapeDtypeStruct((b, n, KV_WIDTH_B), BF16),
           jax.ShapeDtypeStruct((b, N_KV_B, n // tm, GQA_VT_ROWS, tm), BF16)],
        compiler_params=_params("parallel", "parallel"),
        name="proj",
    )(x, w_in.astype(BF16), ca, sa, cb, sb, gq, gk, gsum)


def _dilated_kernel(q_ref, k_ref, v_ref, o_ref, num_ref, den_ref, max_ref, bias_ref, s_ref, top_ref):
    n = q_ref.shape[0]
    last = len(DILATED_PATTERNS) - 1
    for idx, (window, d) in enumerate(sorted(DILATED_PATTERNS, key=lambda wd: -wd[1])):
        half = window // (2 * d)
        length = n // d
        tq = min(BAND_Q, length)
        win = min(tq + 2 * half, length)
        per_phase = length // tq
        lane = lax.broadcasted_iota(jnp.int32, (tq, LANES), 1)
        low = lane < HEAD_DIM
        vlane = lax.broadcasted_iota(jnp.int32, (win, LANES), 1)
        vlow = jnp.where(vlane < HEAD_DIM, 1.0, 0.0).astype(BF16)
        vhigh = jnp.where(vlane < HEAD_DIM, 0.0, 1.0).astype(BF16)
        qi = lax.broadcasted_iota(jnp.int32, (tq, win), 0)
        kj = lax.broadcasted_iota(jnp.int32, (tq, win), 1)
        for case, off in enumerate((0, -half, tq - win)):
            bias_ref[case, :tq, :win] = jnp.where(jnp.abs(kj + off - qi) <= half, 0.0, -jnp.inf)

        total = d * per_phase

        def place(t, d=d, half=half, length=length, tq=tq, win=win, per_phase=per_phase):
            r = t // per_phase
            q0 = (t % per_phase) * tq
            k0 = jnp.clip(q0 - half, 0, length - win)
            case = jnp.where(q0 < half, 0, jnp.where(q0 - half > length - win, 2, 1))
            return pl.ds(r + d * q0, tq, stride=d), pl.ds(r + d * k0, win, stride=d), case

        def scores(t, slot, tq=tq, win=win, low=low, place=place):
            rows, keys, case = place(t)
            q = q_ref[rows, :].astype(BF16)
            k = k_ref[keys, :].astype(BF16)
            bias = bias_ref[case, :tq, :win]
            for h, own in enumerate((low, jnp.logical_not(low))):
                s = _dot_nt(jnp.where(own, q, jnp.zeros_like(q)), k) + bias
                s_ref[slot, h, :tq, :win] = s
                top_ref[slot, h, :tq, :] = jnp.broadcast_to(jnp.max(s, axis=-1, keepdims=True), (tq, LANES))

        def finish(t, slot, tq=tq, win=win, low=low, vlow=vlow, vhigh=vhigh, place=place, idx=idx):
            rows, keys, _ = place(t)
            v = v_ref[keys, :].astype(BF16)
            wide = tuple(jnp.concatenate([v * own, own], axis=1) for own in (vlow, vhigh))
            both = None
            for h in range(2):
                top_wide = jnp.tile(top_ref[slot, h, :tq, :], (1, win // LANES))
                part = _dot(jnp.exp2(s_ref[slot, h, :tq, :win] - top_wide).astype(BF16), wide[h])
                both = part if both is None else both + part
            num, den = both[:, :LANES], both[:, LANES:]
            top = jnp.where(low, top_ref[slot, 0, :tq, :], top_ref[slot, 1, :tq, :])
            if idx > 0:
                old_top = max_ref[rows, :]
                new_top = jnp.maximum(old_top, top)
                a_old, a_new = jnp.exp2(old_top - new_top), jnp.exp2(top - new_top)
                num = num_ref[rows, :] * a_old + num * a_new
                den = den_ref[rows, :] * a_old + den * a_new
                top = new_top
            if idx < last:
                num_ref[rows, :] = num
                den_ref[rows, :] = den
                max_ref[rows, :] = top
            else:
                num_ref[rows, :] = num / den

        unroll = min(BAND_UNROLL, total)

        def trip(i, carry, scores=scores, finish=finish, total=total, unroll=unroll):
            for u in range(unroll):
                t = i * unroll + u
                scores(jnp.minimum(t + 1, total - 1), (u + 1) % 2)
                finish(t, u % 2)
            return carry

        scores(0, 0)
        lax.fori_loop(0, total // unroll, trip, 0)
    o_ref[...] = num_ref[...].astype(o_ref.dtype)


def _dilated_attention(qa, ka, va):
    b, n, width = qa.shape
    spec = pl.BlockSpec((None, n, LANES), lambda bi, h: (bi, 0, h))
    tq = max(min(BAND_Q, n // d) for _, d in DILATED_PATTERNS)
    win = max(min(min(BAND_Q, n // d) + w // d, n // d) for w, d in DILATED_PATTERNS)
    assert all((n // d // min(BAND_Q, n // d)) * d % 2 == 0 for _, d in DILATED_PATTERNS)
    return pl.pallas_call(
        _dilated_kernel,
        grid=(b, width // LANES),
        in_specs=[spec, spec, spec],
        out_specs=spec,
        out_shape=jax.ShapeDtypeStruct((b, n, width), BF16),
        scratch_shapes=[pltpu.VMEM((n, LANES), F32)] * 3 + [pltpu.VMEM((3, tq, win), F32),
                        pltpu.VMEM((2, 2, tq, win), F32), pltpu.VMEM((2, 2, tq, LANES), F32)],
        compiler_params=_params("parallel", "parallel"),
        name="dilated",
    )(qa, ka, va)


def _gqa_kernel(q_ref, qn_ref, k_ref, vt_ref, o_ref, s_ref, top_ref):
    tq = q_ref.shape[0]
    chunks, _, tk = vt_ref.shape
    qs = (q_ref[:, :LANES], q_ref[:, LANES:])
    qn = (qn_ref[:, :LANES], qn_ref[:, LANES:])

    def scores(c, slot, queries=qs):
        start = c * tk if isinstance(c, int) else pl.multiple_of(c * tk, tk)
        k = k_ref[pl.ds(start, tk), :]
        tops = []
        for h, q in enumerate(queries):
            s = _dot_nt(k, q)
            s_ref[slot, h] = s
            tops.append(jnp.max(s, axis=0, keepdims=True))
        return tuple(tops)

    @pl.when(jnp.logical_and(pl.program_id(1) == 0, pl.program_id(2) == 0))
    def _():
        for h, top in enumerate(scores(0, 0)):
            top_ref[h] = top

    def accumulate(c, slot, tops, state):
        vt = vt_ref[c]
        new = []
        for h, (m, acc) in enumerate(state):
            m_new = jnp.maximum(m, tops[h])
            p = jnp.exp2(s_ref[slot, h] - m_new).astype(BF16)
            new.append((m_new, jnp.exp2(m - m_new) * acc + _dot(vt, p)))
        return tuple(new)

    unroll = min(GQA_UNROLL, chunks)

    def trip(t, carry):
        tops, state = carry
        for u in range(unroll):
            c = t * unroll + u
            if isinstance(c, int) and c == chunks - 1:
                nxt = scores(0, 0, qn)
            else:
                nxt = scores(c + 1, (u + 1) % 2)
            state = accumulate(c, u % 2, tops, state)
            tops = nxt
        return tops, state

    state = ((jnp.full((1, tq), -jnp.inf, F32), jnp.zeros((GQA_VT_ROWS, tq), F32)),) * 2
    trips = chunks // unroll
    carry = lax.fori_loop(0, trips - 1, trip, ((top_ref[0], top_ref[1]), state))
    tops, carry = trip(trips - 1, carry)
    for h, top in enumerate(tops):
        top_ref[h] = top
    outs = [acc[:HEAD_DIM] / acc[HEAD_DIM:HEAD_DIM + 1] for _, acc in carry]
    o_ref[...] = jnp.concatenate(outs, axis=0).T.astype(o_ref.dtype)


def _gqa_attention(qb, kb, vt):
    b, n, _ = qb.shape
    _, _, chunks, _, tk = vt.shape
    assert chunks % 2 == 0 and chunks % min(GQA_UNROLL, chunks) == 0, "slots alternate by chunk parity"
    tq = min(GQA_Q, n)
    pairs_per_kv = (N_HEADS_B // N_KV_B) // 2
    ni, nj = n // tq, WIDTH_B // LANES

    def next_step(bi, i, j):
        t = jnp.minimum((bi * ni + i) * nj + j + 1, b * ni * nj - 1)
        return (t // (ni * nj), (t // nj) % ni, t % nj)

    return pl.pallas_call(
        _gqa_kernel,
        grid=(b, ni, nj),
        in_specs=[pl.BlockSpec((None, tq, 2 * LANES), lambda bi, i, j: (bi, i, j)),
                  pl.BlockSpec((None, tq, 2 * LANES), next_step),
                  pl.BlockSpec((None, n, KV_WIDTH_B), lambda bi, i, j: (bi, 0, 0)),
                  pl.BlockSpec((None, None, chunks, GQA_VT_ROWS, tk),
                               lambda bi, i, j: (bi, j // pairs_per_kv, 0, 0, 0))],
        out_specs=pl.BlockSpec((None, tq, LANES), lambda bi, i, j: (bi, i, j)),
        out_shape=jax.ShapeDtypeStruct((b, n, WIDTH_B), BF16),
        scratch_shapes=[pltpu.VMEM((2, 2, tk, tq), F32), pltpu.VMEM((2, 1, tq), F32)],
        compiler_params=_params("arbitrary", "arbitrary", "arbitrary"),
        name="gqa",
    )(qb, qb, kb, vt)


def _memkv_kernel(mem_ref, wk_ref, wv_ref, k_ref, v_ref):
    mb = mem_ref[...].astype(BF16)
    k_ref[...] = _dot(mb, wk_ref[...]).astype(BF16)
    v_ref[...] = _dot(mb, wv_ref[...]).astype(BF16)


def _memory_kv(mem, wk, wv):
    b, m, _ = mem.shape
    spec = pl.BlockSpec((None, m, D_MODEL), lambda bi: (bi, 0, 0))
    wspec = _const_spec((D_MODEL, D_MODEL))
    return pl.pallas_call(
        _memkv_kernel,
        grid=(b,),
        in_specs=[spec, wspec, wspec],
        out_specs=[spec, spec],
        out_shape=[jax.ShapeDtypeStruct((b, m, D_MODEL), BF16)] * 2,
        compiler_params=_params("parallel"),
        name="memkv",
    )(mem, wk.astype(BF16), wv.astype(BF16))


def _layer_norm(x, g, b):
    mu = jnp.mean(x, axis=-1, keepdims=True)
    xc = x - mu
    var = jnp.mean(xc * xc, axis=-1, keepdims=True)
    return xc * lax.rsqrt(var + LN_EPS) * g + b


def _rms_norm(x, g):
    ms = jnp.mean(x * x, axis=-1, keepdims=True)
    return x * lax.rsqrt(ms + RMS_EPS) * g


def _mix_kernel(x_ref, oa_ref, ob_ref,
                ga_ref, gb_ref, wo_ref, g1_ref, b1_ref, wq_ref, km_ref, vm_ref, wco_ref,
                g2_ref, b2_ref, y_ref, *, alpha):
    oa = _rms_norm(oa_ref[...].astype(F32), ga_ref[...]).astype(BF16)
    ob = _rms_norm(ob_ref[...].astype(F32), gb_ref[...]).astype(BF16)
    mix = _dot(oa, wo_ref[:WIDTH_A, :]) + _dot(ob, wo_ref[WIDTH_A:, :])
    x1 = _layer_norm(alpha * x_ref[...] + mix, g1_ref[...], b1_ref[...])

    q = (_dot(x1.astype(BF16), wq_ref[...]) * (HEAD_DIM_MEM ** -0.5)).astype(BF16)
    heads = []
    for h in range(N_HEADS_MEM):
        sl = slice(h * HEAD_DIM_MEM, (h + 1) * HEAD_DIM_MEM)
        s = _dot_nt(q[:, sl], km_ref[:, sl])
        e = jnp.exp(s - jnp.max(s, axis=-1, keepdims=True))
        den = jnp.sum(e, axis=-1, keepdims=True)
        heads.append((_dot(e.astype(BF16), vm_ref[:, sl]) / den).astype(BF16))
    cross = _dot(jnp.concatenate(heads, axis=-1), wco_ref[...])
    y_ref[...] = _layer_norm(alpha * x1 + cross, g2_ref[...], b2_ref[...])


def _mix_cross(x, oa, ob, kmem, vmem, out_norm_a, out_norm_b, w_o, ln1_g, ln1_b,
               wc_q, wc_o, ln2_g, ln2_b, alpha):
    b, n, _ = x.shape
    tm = min(ROW_TILE, n)
    m = kmem.shape[1]

    def rows(width):
        return pl.BlockSpec((None, tm, width), lambda bi, i: (bi, i, 0))

    vec = lambda width: _const_spec((1, width))
    memspec = pl.BlockSpec((None, m, D_MODEL), lambda bi, i: (bi, 0, 0))
    wspec = _const_spec((D_MODEL, D_MODEL))
    return pl.pallas_call(
        functools.partial(_mix_kernel, alpha=alpha),
        grid=(b, n // tm),
        in_specs=[rows(D_MODEL), rows(WIDTH_A), rows(WIDTH_B), vec(WIDTH_A), vec(WIDTH_B),
                  wspec, vec(D_MODEL), vec(D_MODEL), wspec, memspec, memspec, wspec,
                  vec(D_MODEL), vec(D_MODEL)],
        out_specs=rows(D_MODEL),
        out_shape=jax.ShapeDtypeStruct((b, n, D_MODEL), F32),
        compiler_params=_params("parallel", "parallel"),
        name="mix",
    )(x, oa, ob, out_norm_a[None, :], out_norm_b[None, :], w_o.astype(BF16),
      ln1_g[None, :], ln1_b[None, :], wc_q.astype(BF16), kmem, vmem, wc_o.astype(BF16),
      ln2_g[None, :], ln2_b[None, :])


def _ffn_kernel(x_ref, prev_ref, next_ref, wup_ref, cw_ref, cb_ref, wdn_ref, g_ref, b_ref,
                y_ref, h_ref, *, alpha):
    rows = x_ref.shape[0]
    i = pl.program_id(1)
    x = x_ref[...]
    prev = jnp.where(i > 0, prev_ref[...], 0.0)
    nxt = jnp.where(i < pl.num_programs(1) - 1, next_ref[...], 0.0)
    xb = x.astype(BF16)
    xe = jnp.concatenate([prev, x, nxt], axis=0).astype(BF16)
    ext = rows + 2 * SUBLANES
    for c in range(D_FF // FF_CHUNK):
        sl = slice(c * FF_CHUNK, (c + 1) * FF_CHUNK)
        gate = _dot(xe, wup_ref[:, sl])
        before = pltpu.roll(gate, 1, axis=0)[SUBLANES:SUBLANES + rows]
        after = pltpu.roll(gate, ext - 1, axis=0)[SUBLANES:SUBLANES + rows]
        here = gate[SUBLANES:SUBLANES + rows]
        conv = (before * cw_ref[0:1, sl] + here * cw_ref[1:2, sl] + after * cw_ref[2:3, sl]
                + cb_ref[:, sl])
        act = 0.5 * conv * (1.0 + lax.erf(conv * (2.0 ** -0.5)))
        val = _dot(xb, wup_ref[:, D_FF + c * FF_CHUNK:D_FF + (c + 1) * FF_CHUNK])
        h_ref[:, sl] = (act * val).astype(BF16)
    y = _dot(h_ref[...], wdn_ref[...])
    y_ref[...] = _layer_norm(alpha * x + y, g_ref[...], b_ref[...])


def _conv_glu_ffn(x, w_up, conv_w, conv_b, w_down, ln_g, ln_b, alpha):
    b, n, _ = x.shape
    tm = min(ROW_TILE, n)
    per = tm // SUBLANES
    last = n // SUBLANES - 1
    vec = _const_spec((1, D_MODEL))
    return pl.pallas_call(
        functools.partial(_ffn_kernel, alpha=alpha),
        grid=(b, n // tm),
        in_specs=[pl.BlockSpec((None, tm, D_MODEL), lambda bi, i: (bi, i, 0)),
                  pl.BlockSpec((None, SUBLANES, D_MODEL),
                               lambda bi, i: (bi, jnp.maximum(i * per - 1, 0), 0)),
                  pl.BlockSpec((None, SUBLANES, D_MODEL),
                               lambda bi, i: (bi, jnp.minimum((i + 1) * per, last), 0)),
                  _const_spec((D_MODEL, 2 * D_FF)), _const_spec((3, D_FF)), _const_spec((1, D_FF)),
                  _const_spec((D_FF, D_MODEL)), vec, vec],
        out_specs=pl.BlockSpec((None, tm, D_MODEL), lambda bi, i: (bi, i, 0)),
        out_shape=jax.ShapeDtypeStruct((b, n, D_MODEL), F32),
        scratch_shapes=[pltpu.VMEM((tm, D_FF), BF16)],
        compiler_params=_params("parallel", "parallel"),
        name="ffn",
    )(x, x, x, w_up.astype(BF16), conv_w, conv_b[None, :], w_down.astype(BF16),
      ln_g[None, :], ln_b[None, :])


def _encoder(x, mem, w_in, q_norm_g, k_norm_g, out_norm_a, out_norm_b, w_o, ln1_g, ln1_b,
             wc_q, wc_k, wc_v, wc_o, ln2_g, ln2_b, w_up, conv_w, conv_b, w_down, ln3_g, ln3_b):
    depth = w_in.shape[0]
    alpha = (2 * depth) ** 0.25
    for l in range(depth):
        qa, ka, va, qb, kb, vb = _project(x, w_in[l], q_norm_g[l], k_norm_g[l])
        oa = _dilated_attention(qa, ka, va)
        ob = _gqa_attention(qb, kb, vb)
        kmem, vmem = _memory_kv(mem, wc_k[l], wc_v[l])
        x = _mix_cross(x, oa, ob, kmem, vmem, out_norm_a[l], out_norm_b[l], w_o[l],
                       ln1_g[l], ln1_b[l], wc_q[l], wc_o[l], ln2_g[l], ln2_b[l], alpha)
        x = _conv_glu_ffn(x, w_up[l], conv_w[l], conv_b[l], w_down[l], ln3_g[l], ln3_b[l], alpha)
    return x


def kernel(x_prompt, x_sample, mem_prompt, mem_sample, w_in, q_norm_g, k_norm_g, out_norm_a, out_norm_b, w_o, ln1_g, ln1_b, wc_q, wc_k, wc_v, wc_o, ln2_g, ln2_b, w_up, conv_w, conv_b, w_down, ln3_g, ln3_b):
    weights = (w_in, q_norm_g, k_norm_g, out_norm_a, out_norm_b, w_o, ln1_g, ln1_b,
               wc_q, wc_k, wc_v, wc_o, ln2_g, ln2_b, w_up, conv_w, conv_b, w_down, ln3_g, ln3_b)
    return (_encoder(x_prompt, mem_prompt, *weights), _encoder(x_sample, mem_sample, *weights))
```
